```python
import jax, jax.numpy as jnp
from jax import lax
import numpy as np

D_MODEL = 1024
BATCH = 32
SEQ = 256
DEPTH = 2
DEC_BATCH = 8
DEC_SEQ = 1024
PAST_LEN = 256

GRID_W = 64
CHUNK = 128
A_GROUPS = 4
A_WIDTH = D_MODEL // 2
A_GROUP_DIM = A_WIDTH // A_GROUPS
B_GROUPS = 4
B_WIDTH = D_MODEL // 2
B_GROUP_DIM = B_WIDTH // B_GROUPS
HEAD_DIM = 128
N_HEADS = D_MODEL // HEAD_DIM
N_KV_HEADS = 2
KV_GROUP = N_HEADS // N_KV_HEADS
WINDOW = 128
Q_BLOCK = 128
BAND = Q_BLOCK + 2 * WINDOW
AXIS_DIM = HEAD_DIM // 2
ROPE_BASE = 10000.0
D_FF = ((8 * D_MODEL + 3 * 256 - 1) // (3 * 256)) * 256
N_EVEN = (DEPTH + 1) // 2
N_ODD = DEPTH // 2
EPS = 1e-6
NEG = -1e30

kernel_name = "hybrid_dit_gmlp_fnet_swa_step"


def rms_norm(x, g):
    xf = x.astype(jnp.float32)
    y = xf * lax.rsqrt(jnp.mean(xf * xf, axis=-1, keepdims=True) + EPS)
    return (y * g.astype(jnp.float32)).astype(x.dtype)


def layer_norm(x, g):
    xf = x.astype(jnp.float32)
    xc = xf - jnp.mean(xf, axis=-1, keepdims=True)
    y = xc * lax.rsqrt(jnp.mean(xc * xc, axis=-1, keepdims=True) + EPS)
    return (y * g.astype(jnp.float32)).astype(x.dtype)


def modulation(cond, w, b):
    m = jax.nn.silu(cond) @ w + b
    return [p[:, None, :] for p in jnp.split(m, 6, axis=-1)]


def modulate(h, shift, scale):
    return h * (1 + scale) + shift


def chunk_gmlp(u, v, sgu_w, sgu_b, sgu_g):
    bn, t, _ = v.shape
    v = layer_norm(v, sgu_g)
    vc = v.reshape(bn, t // CHUNK, CHUNK, A_GROUPS, A_GROUP_DIM)
    mixed = jnp.einsum('hpq,bnqhc->bnphc', sgu_w, vc) + sgu_b.T[None, None, :, :, None]
    return u * mixed.reshape(bn, t, A_WIDTH)


def fourier_mix(z):
    bn, t, _ = z.shape
    zg = z.reshape(bn, t, B_GROUPS, B_GROUP_DIM).astype(jnp.float32)
    f = jnp.fft.fft2(zg, axes=(1, 3), norm='ortho')
    return jnp.real(f).reshape(bn, t, B_WIDTH).astype(z.dtype)


def mixer_ab(h, w_in, sgu_w, sgu_b, sgu_g, w_out):
    z = h @ w_in
    u, v, zb = jnp.split(z, [A_WIDTH, 2 * A_WIDTH], axis=-1)
    a = chunk_gmlp(jax.nn.gelu(u), jax.nn.gelu(v), sgu_w, sgu_b, sgu_g)
    b = fourier_mix(zb)
    return jnp.concatenate([a, b], axis=-1) @ w_out


def qkv_proj(h, w_qkv):
    bn, t, _ = h.shape
    z = h @ w_qkv
    q, k, v = jnp.split(z, [N_HEADS * HEAD_DIM, (N_HEADS + N_KV_HEADS) * HEAD_DIM], axis=-1)
    return (q.reshape(bn, t, N_KV_HEADS, KV_GROUP, HEAD_DIM),
            k.reshape(bn, t, N_KV_HEADS, HEAD_DIM),
            v.reshape(bn, t, N_KV_HEADS, HEAD_DIM))


def axial_rope_tables(t):
    rows_n = t // GRID_W
    rows = jnp.repeat(jnp.arange(rows_n), GRID_W).astype(jnp.float32)
    cols = jnp.tile(jnp.arange(GRID_W), rows_n).astype(jnp.float32)
    inv = ROPE_BASE ** (-jnp.arange(0, AXIS_DIM, 2, dtype=jnp.float32) / AXIS_DIM)
    ar = rows[:, None] * inv
    ac = cols[:, None] * inv
    ang = jnp.concatenate([ar, ar, ac, ac], axis=-1)
    return jnp.cos(ang), jnp.sin(ang)


def apply_rope(x, cos, sin):
    xf = x.astype(jnp.float32)
    xs = xf.reshape(xf.shape[:-1] + (2, 2, AXIS_DIM // 2))
    rot = jnp.stack([-xs[..., 1, :], xs[..., 0, :]], axis=-2).reshape(xf.shape)
    bshape = (1, x.shape[1]) + (1,) * (x.ndim - 3) + (HEAD_DIM,)
    return (xf * cos.reshape(bshape) + rot * sin.reshape(bshape)).astype(x.dtype)


def sink_attend(q, ks, vs, masks, sink):
    scale = HEAD_DIM ** -0.5
    scores = []
    for k, m in zip(ks, masks):
        s = jnp.einsum('bqkgd,bskd->bkgqs', q, k).astype(jnp.float32) * scale
        if m is not None:
            s = jnp.where(m, s, NEG)
        scores.append(s)
    bn, nq = q.shape[0], q.shape[1]
    sink_col = jnp.broadcast_to(sink.astype(jnp.float32).reshape(1, N_KV_HEADS, KV_GROUP, 1, 1),
                                (bn, N_KV_HEADS, KV_GROUP, nq, 1))
    p = jax.nn.softmax(jnp.concatenate(scores + [sink_col], axis=-1), axis=-1)
    out = None
    off = 0
    for v in vs:
        n = v.shape[1]
        o = jnp.einsum('bkgqs,bskd->bqkgd', p[..., off:off + n].astype(v.dtype), v)
        out = o if out is None else out + o
        off += n
    return out


def split_query_blocks(q):
    bn, t = q.shape[:2]
    nb = t // Q_BLOCK
    return q.reshape(bn, nb, Q_BLOCK, N_KV_HEADS, KV_GROUP, HEAD_DIM).transpose(1, 0, 2, 3, 4, 5)


def merge_query_blocks(o, bn, t):
    return o.transpose(1, 0, 2, 3, 4, 5).reshape(bn, t, N_HEADS * HEAD_DIM)


def context_attention(q, k, v, sink):
    bn, s = q.shape[:2]
    out = lax.map(lambda qi: sink_attend(qi, [k], [v], [None], sink), split_query_blocks(q))
    return merge_query_blocks(out, bn, s)


def latent_attention(q, k, v, ck, cv, sink):
    bn, t = q.shape[:2]
    nb = t // Q_BLOCK
    pad = ((0, 0), (WINDOW, WINDOW), (0, 0), (0, 0))
    kp = jnp.pad(k, pad)
    vp = jnp.pad(v, pad)

    def block(args):
        qi, j = args
        start = j * Q_BLOCK
        kb = lax.dynamic_slice_in_dim(kp, start, BAND, axis=1)
        vb = lax.dynamic_slice_in_dim(vp, start, BAND, axis=1)
        qpos = start + jnp.arange(Q_BLOCK)
        kpos = start - WINDOW + jnp.arange(BAND)
        mask = ((kpos >= 0) & (kpos < t))[None, :] & (jnp.abs(qpos[:, None] - kpos[None, :]) <= WINDOW)
        return sink_attend(qi, [kb, ck], [vb, cv], [mask, None], sink)

    out = lax.map(block, (split_query_blocks(q), jnp.arange(nb)))
    return merge_query_blocks(out, bn, t)


def swiglu(h, w_gate, w_up, w_down):
    return (jax.nn.silu(h @ w_gate) * (h @ w_up)) @ w_down


def _normal(k, shape, scale):
    return jax.random.normal(k, shape, jnp.float32) * scale


def setup_inputs(seed: int = 0) -> dict:
    key = jax.random.key(seed)
    ks = jax.random.split(key, 24)
    d = D_MODEL
    qkv_w = (N_HEADS + 2 * N_KV_HEADS) * HEAD_DIM
    return {
        'x_prompt': _normal(ks[0], (BATCH, SEQ, d), 1.0),
        'x_sample': _normal(ks[1], (DEC_BATCH, DEC_SEQ, d), 1.0),
        'cache_k': _normal(ks[2], (DEC_BATCH, N_ODD, PAST_LEN, N_KV_HEADS, HEAD_DIM), 1.0),
        'cache_v': _normal(ks[3], (DEC_BATCH, N_ODD, PAST_LEN, N_KV_HEADS, HEAD_DIM), 1.0),
        'c': _normal(ks[4], (DEC_BATCH, d), 1.0),
        'c_ctx': _normal(ks[5], (d,), 1.0),
        'mod_w': _normal(ks[6], (DEPTH, d, 6 * d), 0.5 * d ** -0.5),
        'mod_b': _normal(ks[7], (DEPTH, 6 * d), 0.02),
        'norm_pre_mix': 1.0 + _normal(ks[8], (DEPTH, d), 0.02),
        'norm_post_mix': 1.0 + _normal(ks[9], (DEPTH, d), 0.02),
        'norm_pre_ffn': 1.0 + _normal(ks[10], (DEPTH, d), 0.02),
        'norm_post_ffn': 1.0 + _normal(ks[11], (DEPTH, d), 0.02),
        'ab_w_in': _normal(ks[12], (N_EVEN, d, 2 * A_WIDTH + B_WIDTH), d ** -0.5),
        'sgu_w': _normal(ks[13], (N_EVEN, A_GROUPS, CHUNK, CHUNK), CHUNK ** -0.5),
        'sgu_b': 1.0 + _normal(ks[14], (N_EVEN, A_GROUPS, CHUNK), 0.02),
        'sgu_g': 1.0 + _normal(ks[15], (N_EVEN, A_WIDTH), 0.02),
        'ab_w_out': _normal(ks[16], (N_EVEN, A_WIDTH + B_WIDTH, d), (A_WIDTH + B_WIDTH) ** -0.5),
        'attn_w_qkv': _normal(ks[17], (N_ODD, d, qkv_w), d ** -0.5),
        'attn_sink': _normal(ks[18], (N_ODD, N_HEADS), 0.5),
        'attn_w_o': _normal(ks[19], (N_ODD, N_HEADS * HEAD_DIM, d), (N_HEADS * HEAD_DIM) ** -0.5),
        'ffn_w_gate': _normal(ks[20], (DEPTH, d, D_FF), d ** -0.5),
        'ffn_w_up': _normal(ks[21], (DEPTH, d, D_FF), d ** -0.5),
        'ffn_w_down': _normal(ks[22], (DEPTH, D_FF, d), D_FF ** -0.5),
    }


def reference(x_prompt, x_sample, cache_k, cache_v, c, c_ctx, mod_w, mod_b,
              norm_pre_mix, norm_post_mix, norm_pre_ffn, norm_post_ffn,
              ab_w_in, sgu_w, sgu_b, sgu_g, ab_w_out,
              attn_w_qkv, attn_sink, attn_w_o,
              ffn_w_gate, ffn_w_up, ffn_w_down):
    t = x_sample.shape[1]
    cos, sin = axial_rope_tables(t)
    xp, xs = x_prompt, x_sample
    new_k, new_v = [], []
    for layer in range(DEPTH):
        mp = modulation(c_ctx[None, :], mod_w[layer], mod_b[layer])
        msm = modulation(c, mod_w[layer], mod_b[layer])
        hp = modulate(rms_norm(xp, norm_pre_mix[layer]), mp[0], mp[1])
        hs = modulate(rms_norm(xs, norm_pre_mix[layer]), msm[0], msm[1])
        if layer % 2 == 0:
            e = layer // 2
            op = mixer_ab(hp, ab_w_in[e], sgu_w[e], sgu_b[e], sgu_g[e], ab_w_out[e])
            osm = mixer_ab(hs, ab_w_in[e], sgu_w[e], sgu_b[e], sgu_g[e], ab_w_out[e])
        else:
            o = layer // 2
            qp, kp, vp = qkv_proj(hp, attn_w_qkv[o])
            new_k.append(kp)
            new_v.append(vp)
            op = context_attention(qp, kp, vp, attn_sink[o]) @ attn_w_o[o]
            qs, kl, vl = qkv_proj(hs, attn_w_qkv[o])
            qs = apply_rope(qs, cos, sin)
            kl = apply_rope(kl, cos, sin)
            osm = latent_attention(qs, kl, vl, cache_k[:, o], cache_v[:, o], attn_sink[o]) @ attn_w_o[o]
        xp = xp + mp[2] * rms_norm(op, norm_post_mix[layer])
        xs = xs + msm[2] * rms_norm(osm, norm_post_mix[layer])
        hp = modulate(rms_norm(xp, norm_pre_ffn[layer]), mp[3], mp[4])
        hs = modulate(rms_norm(xs, norm_pre_ffn[layer]), msm[3], msm[4])
        fp = swiglu(hp, ffn_w_gate[layer], ffn_w_up[layer], ffn_w_down[layer])
        fs = swiglu(hs, ffn_w_gate[layer], ffn_w_up[layer], ffn_w_down[layer])
        xp = xp + mp[5] * rms_norm(fp, norm_post_ffn[layer])
        xs = xs + msm[5] * rms_norm(fs, norm_post_ffn[layer])
    state_k = jnp.stack(new_k, axis=1)
    state_v = jnp.stack(new_v, axis=1)
    return (xp, xs, state_k, state_v)
```

```python
import functools

import numpy as np
import jax
import jax.numpy as jnp
from jax import lax
from jax.experimental import pallas as pl
from jax.experimental.pallas import tpu as pltpu

F32 = jnp.float32
BF16 = jnp.bfloat16

D_MODEL = 1024
GRID_W = 64
CHUNK = 128
A_GROUPS = 4
A_WIDTH = D_MODEL // 2
B_GROUPS = 4
B_WIDTH = D_MODEL // 2
B_GROUP_DIM = B_WIDTH // B_GROUPS
HEAD_DIM = 128
N_HEADS = D_MODEL // HEAD_DIM
N_KV_HEADS = 2
KV_GROUP = N_HEADS // N_KV_HEADS
KV_WIDTH = N_KV_HEADS * HEAD_DIM
WINDOW = 128
Q_BLOCK = 128
AXIS_DIM = HEAD_DIM // 2
ROPE_BASE = 10000.0
EPS = 1e-6
NEG = -1e30

ROW_TILE = 1024
ROW_BLOCK = 256
FFN_TILE = 512
FFN_CHUNK = 768
MOD_ROWS = 16
MOD_COLS = 1536
VMEM_LIMIT = 56 * 1024 * 1024


def _dot(a, b):
    return jnp.dot(a, b, preferred_element_type=F32)


def _dot_t(a, b):
    return lax.dot_general(a, b, (((1,), (1,)), ((), ())), preferred_element_type=F32)


def _rms(x, g):
    ms = jnp.mean(x * x, axis=-1, keepdims=True)
    return x * lax.rsqrt(ms + EPS) * g


def _const_spec(shape):
    nd = len(shape)
    return pl.BlockSpec(shape, lambda i, _nd=nd: (0,) * _nd, pipeline_mode=pl.Buffered(1))


def _mod_kernel(cond_ref, w_ref, b_ref, o_ref):
    a = jax.nn.silu(cond_ref[...]).astype(BF16)
    o_ref[0] = _dot(a, w_ref[0].astype(BF16)) + b_ref[0]


def _mod_call(cond, mod_w, mod_b):
    depth, d, n = mod_w.shape
    return pl.pallas_call(
        _mod_kernel,
        grid=(depth, n // MOD_COLS),
        in_specs=[
            pl.BlockSpec((MOD_ROWS, d), lambda l, j: (0, 0)),
            pl.BlockSpec((1, d, MOD_COLS), lambda l, j: (l, 0, j)),
            pl.BlockSpec((1, 1, MOD_COLS), lambda l, j: (l, 0, j)),
        ],
        out_specs=pl.BlockSpec((1, MOD_ROWS, MOD_COLS), lambda l, j: (l, 0, j)),
        out_shape=jax.ShapeDtypeStruct((depth, MOD_ROWS, n), F32),
        compiler_params=pltpu.CompilerParams(
            dimension_semantics=("arbitrary", "arbitrary"), vmem_limit_bytes=VMEM_LIMIT),
        name="modulation",
    )(cond, mod_w, mod_b.reshape(depth, 1, n))


def _dft_tables(t):
    c = B_GROUP_DIM
    kc = (np.arange(c)[:, None] * np.arange(c)[None, :]) % c
    ac = 2.0 * np.pi * kc / c
    fc = np.concatenate([np.cos(ac), np.sin(ac)], axis=1)
    kt = (np.arange(t)[:, None] * np.arange(t)[None, :]) % t
    at = 2.0 * np.pi * kt / t
    ft = np.concatenate([np.cos(at), -np.sin(at)], axis=1) / np.sqrt(float(t * c))
    return jnp.asarray(fc, dtype=F32).astype(BF16), jnp.asarray(ft, dtype=F32).astype(BF16)


def _mixer_ab_kernel(x_ref, mod_ref, npre_ref, npost_ref, win_ref, sguw_ref, sgub_ref, sgug_ref,
                     fc_ref, ft_ref, wout_ref, o_ref,
                     u_scr, v_scr, zb_scr, ab_scr, zcs_scr, *, seq_len):
    rows_total = x_ref.shape[0]
    nseq = rows_total // seq_len
    shift, scale, gate = mod_ref[0, 0:1, :], mod_ref[0, 1:2, :], mod_ref[0, 2:3, :]

    for r in range(rows_total // ROW_BLOCK):
        rows = pl.ds(r * ROW_BLOCK, ROW_BLOCK)
        h = (_rms(x_ref[rows, :], npre_ref[...]) * (1.0 + scale) + shift).astype(BF16)
        z = _dot(h, win_ref[...])
        u_scr[rows, :] = jax.nn.gelu(z[:, :A_WIDTH])
        v = jax.nn.gelu(z[:, A_WIDTH:2 * A_WIDTH])
        vc = v - jnp.mean(v, axis=-1, keepdims=True)
        v = vc * lax.rsqrt(jnp.mean(vc * vc, axis=-1, keepdims=True) + EPS) * sgug_ref[...]
        v_scr[rows, :] = v.astype(BF16)
        zb_scr[rows, :] = z[:, 2 * A_WIDTH:].astype(BF16)

    gw = A_WIDTH // A_GROUPS
    for n in range(rows_total // CHUNK):
        rows = pl.ds(n * CHUNK, CHUNK)
        for g in range(A_GROUPS):
            lanes = slice(g * gw, (g + 1) * gw)
            mixed = _dot(sguw_ref[g], v_scr[rows, lanes]) + sgub_ref[:, lanes]
            ab_scr[rows, lanes] = (u_scr[rows, lanes] * mixed).astype(BF16)

    c = B_GROUP_DIM
    for s in range(nseq):
        rows = pl.ds(s * seq_len, seq_len)
        for g in range(B_GROUPS):
            lanes = slice(g * c, (g + 1) * c)
            zz = _dot(zb_scr[rows, lanes], fc_ref[...])
            zcs_scr[s, 0:seq_len, lanes] = zz[:, :c].astype(BF16)
            zcs_scr[s, seq_len:2 * seq_len, lanes] = zz[:, c:].astype(BF16)
        ab_scr[rows, A_WIDTH:] = _dot(ft_ref[...], zcs_scr[s]).astype(BF16)

    for r in range(rows_total // ROW_BLOCK):
        rows = pl.ds(r * ROW_BLOCK, ROW_BLOCK)
        y = _dot(ab_scr[rows, :], wout_ref[...])
        o_ref[rows, :] = x_ref[rows, :] + gate * _rms(y, npost_ref[...])


def _mixer_ab_call(x, mods, per_tile_mod, npre, npost, win, sguw, sgub_full, sgug, wout, seq_len):
    n_tok, d = x.shape
    fc, ft = _dft_tables(seq_len)
    nseq = ROW_TILE // seq_len
    mod_map = (lambda i: (i, 0, 0)) if per_tile_mod else (lambda i: (0, 0, 0))
    return pl.pallas_call(
        functools.partial(_mixer_ab_kernel, seq_len=seq_len),
        grid=(n_tok // ROW_TILE,),
        in_specs=[
            pl.BlockSpec((ROW_TILE, d), lambda i: (i, 0)),
            pl.BlockSpec((1, 6, d), mod_map),
            _const_spec(npre.shape), _const_spec(npost.shape), _const_spec(win.shape),
            _const_spec(sguw.shape), _const_spec(sgub_full.shape), _const_spec(sgug.shape),
            _const_spec(fc.shape), _const_spec(ft.shape), _const_spec(wout.shape),
        ],
        out_specs=pl.BlockSpec((ROW_TILE, d), lambda i: (i, 0)),
        out_shape=jax.ShapeDtypeStruct((n_tok, d), F32),
        scratch_shapes=[
            pltpu.VMEM((ROW_TILE, A_WIDTH), F32),
            pltpu.VMEM((ROW_TILE, A_WIDTH), BF16),
            pltpu.VMEM((ROW_TILE, B_WIDTH), BF16),
            pltpu.VMEM((ROW_TILE, A_WIDTH + B_WIDTH), BF16),
            pltpu.VMEM((nseq, 2 * seq_len, B_WIDTH), BF16),
        ],
        compiler_params=pltpu.CompilerParams(
            dimension_semantics=("arbitrary",), vmem_limit_bytes=VMEM_LIMIT),
        name="mixer_ab",
    )(x, mods, npre, npost, win, sguw, sgub_full, sgug, fc, ft, wout)


def _rope_tables(t):
    rows_n = t // GRID_W
    rows = jnp.repeat(jnp.arange(rows_n), GRID_W).astype(F32)
    cols = jnp.tile(jnp.arange(GRID_W), rows_n).astype(F32)
    inv = ROPE_BASE ** (-jnp.arange(0, AXIS_DIM, 2, dtype=F32) / AXIS_DIM)
    ar = rows[:, None] * inv
    ac = cols[:, None] * inv
    ang = jnp.concatenate([ar, ar, ac, ac], axis=-1)
    cos, sin = jnp.cos(ang), jnp.sin(ang)
    first_half = (jnp.arange(HEAD_DIM) % AXIS_DIM) < (AXIS_DIM // 2)
    sin_up = jnp.where(first_half[None, :], -sin, 0.0)
    sin_dn = jnp.where(first_half[None, :], 0.0, sin)
    return cos, sin_up, sin_dn


def _softmax_pv(score_parts, value_parts, sink_col):
    m = sink_col
    for s in score_parts:
        m = jnp.maximum(m, jnp.max(s, axis=-1, keepdims=True))
    denom = jnp.exp(sink_col - m)
    out = None
    for s, v in zip(score_parts, value_parts):
        p = jnp.exp(s - m)
        denom = denom + jnp.sum(p, axis=-1, keepdims=True)
        o = _dot(p.astype(BF16), v)
        out = o if out is None else out + o
    return out / denom


def _attn_kernel(*refs, seq_len, latent):
    if latent:
        (sink_ref, x_ref, mod_ref, npre_ref, npost_ref, wqkv_ref, wo_ref,
         cos_ref, sup_ref, sdn_ref, ck_ref, cv_ref, o_ref, q_scr, k_scr, v_scr, a_scr) = refs
    else:
        (sink_ref, x_ref, mod_ref, npre_ref, npost_ref, wqkv_ref, wo_ref,
         o_ref, ko_ref, vo_ref, q_scr, k_scr, v_scr, a_scr) = refs
    rows_total = x_ref.shape[0]
    shift, scale, gate = mod_ref[0, 0:1, :], mod_ref[0, 1:2, :], mod_ref[0, 2:3, :]
    qw = N_HEADS * HEAD_DIM
    sm_scale = HEAD_DIM ** -0.5

    for r in range(rows_total // ROW_BLOCK):
        rows = pl.ds(r * ROW_BLOCK, ROW_BLOCK)
        h = (_rms(x_ref[rows, :], npre_ref[...]) * (1.0 + scale) + shift).astype(BF16)
        z = _dot(h, wqkv_ref[...])
        if latent:
            cos, sup, sdn = cos_ref[rows, :], sup_ref[rows, :], sdn_ref[rows, :]
            for hd in range(N_HEADS + N_KV_HEADS):
                zh = z[:, hd * HEAD_DIM:(hd + 1) * HEAD_DIM]
                zr = (zh * cos + pltpu.roll(zh, HEAD_DIM - AXIS_DIM // 2, axis=1) * sup
                      + pltpu.roll(zh, AXIS_DIM // 2, axis=1) * sdn).astype(BF16)
                if hd < N_HEADS:
                    q_scr[rows, hd * HEAD_DIM:(hd + 1) * HEAD_DIM] = zr
                else:
                    k_scr[rows, (hd - N_HEADS) * HEAD_DIM:(hd - N_HEADS + 1) * HEAD_DIM] = zr
        else:
            q_scr[rows, :] = z[:, :qw].astype(BF16)
            k_scr[rows, :] = z[:, qw:qw + KV_WIDTH].astype(BF16)
            ko_ref[rows, :] = z[:, qw:qw + KV_WIDTH]
            vo_ref[rows, :] = z[:, qw + KV_WIDTH:]
        v_scr[rows, :] = z[:, qw + KV_WIDTH:].astype(BF16)

    def sink_column(kh, rows_per_head):
        return jnp.concatenate(
            [jnp.full((rows_per_head, 1), sink_ref[kh * KV_GROUP + g], F32) for g in range(KV_GROUP)], axis=0)

    def stacked_queries(rows, kh):
        return jnp.concatenate(
            [q_scr[rows, (kh * KV_GROUP + g) * HEAD_DIM:(kh * KV_GROUP + g + 1) * HEAD_DIM]
             for g in range(KV_GROUP)], axis=0)

    def store_heads(rows, kh, out, rows_per_head):
        for g in range(KV_GROUP):
            hd = kh * KV_GROUP + g
            a_scr[rows, hd * HEAD_DIM:(hd + 1) * HEAD_DIM] = (
                out[g * rows_per_head:(g + 1) * rows_per_head].astype(BF16))

    if latent:
        nblk = seq_len // Q_BLOCK
        for j in range(nblk):
            rows = pl.ds(j * Q_BLOCK, Q_BLOCK)
            lo, hi = max(j - 1, 0), min(j + 2, nblk)
            band = pl.ds(lo * Q_BLOCK, (hi - lo) * Q_BLOCK)
            nk = (hi - lo) * Q_BLOCK
            qpos = j * Q_BLOCK + lax.broadcasted_iota(jnp.int32, (KV_GROUP * Q_BLOCK, nk), 0) % Q_BLOCK
            kpos = lo * Q_BLOCK + lax.broadcasted_iota(jnp.int32, (KV_GROUP * Q_BLOCK, nk), 1)
            mask = jnp.abs(qpos - kpos) <= WINDOW
            for kh in range(N_KV_HEADS):
                kl = slice(kh * HEAD_DIM, (kh + 1) * HEAD_DIM)
                qh = stacked_queries(rows, kh)
                s_band = jnp.where(mask, _dot_t(qh, k_scr[band, kl]) * sm_scale, NEG)
                s_ctx = _dot_t(qh, ck_ref[0, :, kl].astype(BF16)) * sm_scale
                out = _softmax_pv([s_band, s_ctx],
                                  [v_scr[band, kl], cv_ref[0, :, kl].astype(BF16)],
                                  sink_column(kh, Q_BLOCK))
                store_heads(rows, kh, out, Q_BLOCK)
    else:
        for s in range(rows_total // seq_len):
            rows = pl.ds(s * seq_len, seq_len)
            for kh in range(N_KV_HEADS):
                kl = slice(kh * HEAD_DIM, (kh + 1) * HEAD_DIM)
                qh = stacked_queries(rows, kh)
                sc = _dot_t(qh, k_scr[rows, kl]) * sm_scale
                out = _softmax_pv([sc], [v_scr[rows, kl]], sink_column(kh, seq_len))
                store_heads(rows, kh, out, seq_len)

    for r in range(rows_total // ROW_BLOCK):
        rows = pl.ds(r * ROW_BLOCK, ROW_BLOCK)
        y = _dot(a_scr[rows, :], wo_ref[...])
        o_ref[rows, :] = x_ref[rows, :] + gate * _rms(y, npost_ref[...])


def _attn_call(x, mods, sink, npre, npost, wqkv, wo, seq_len, cache=None):
    n_tok, d = x.shape
    latent = cache is not None
    row_spec = pl.BlockSpec((ROW_TILE, d), lambda i: (i, 0))
    smem_spec = pl.BlockSpec(memory_space=pltpu.SMEM)
    in_specs = [smem_spec, row_spec,
                pl.BlockSpec((1, 6, d), (lambda i: (i, 0, 0)) if latent else (lambda i: (0, 0, 0))),
                _const_spec(npre.shape), _const_spec(npost.shape),
                _const_spec(wqkv.shape), _const_spec(wo.shape)]
    args = [sink, x, mods, npre, npost, wqkv, wo]
    if latent:
        ck, cv = cache
        cos, sup, sdn = _rope_tables(seq_len)
        cache_spec = pl.BlockSpec((1,) + ck.shape[1:], lambda i: (i, 0, 0))
        in_specs += [_const_spec(cos.shape)] * 3 + [cache_spec, cache_spec]
        args += [cos, sup, sdn, ck, cv]
        out_specs = row_spec
        out_shape = jax.ShapeDtypeStruct((n_tok, d), F32)
    else:
        kv_spec = pl.BlockSpec((ROW_TILE, KV_WIDTH), lambda i: (i, 0))
        out_specs = [row_spec, kv_spec, kv_spec]
        out_shape = [jax.ShapeDtypeStruct((n_tok, d), F32),
                     jax.ShapeDtypeStruct((n_tok, KV_WIDTH), F32),
                     jax.ShapeDtypeStruct((n_tok, KV_WIDTH), F32)]
    return pl.pallas_call(
        functools.partial(_attn_kernel, seq_len=seq_len, latent=latent),
        grid=(n_tok // ROW_TILE,),
        in_specs=in_specs,
        out_specs=out_specs,
        out_shape=out_shape,
        scratch_shapes=[
            pltpu.VMEM((ROW_TILE, N_HEADS * HEAD_DIM), BF16),
            pltpu.VMEM((ROW_TILE, KV_WIDTH), BF16),
            pltpu.VMEM((ROW_TILE, KV_WIDTH), BF16),
            pltpu.VMEM((ROW_TILE, N_HEADS * HEAD_DIM), BF16),
        ],
        compiler_params=pltpu.CompilerParams(
            dimension_semantics=("arbitrary",), vmem_limit_bytes=VMEM_LIMIT),
        name="attn_latent" if latent else "attn_context",
    )(*args)


def _ffn_kernel(x_ref, mod_ref, npre_ref, npost_ref, wg_ref, wu_ref, wd_ref, o_ref, a_scr):
    shift, scale, gate = mod_ref[0, 3:4, :], mod_ref[0, 4:5, :], mod_ref[0, 5:6, :]
    x = x_ref[...]
    h = (_rms(x, npre_ref[...]) * (1.0 + scale) + shift).astype(BF16)
    d_ff = wg_ref.shape[1]
    for c0 in range(0, d_ff, FFN_CHUNK):
        cols = slice(c0, min(c0 + FFN_CHUNK, d_ff))
        g = _dot(h, wg_ref[:, cols])
        u = _dot(h, wu_ref[:, cols])
        a_scr[:, cols] = (jax.nn.silu(g) * u).astype(BF16)
    y = _dot(a_scr[...], wd_ref[...])
    o_ref[...] = x + gate * _rms(y, npost_ref[...])


def _ffn_call(x, mods, tiles_per_mod, npre, npost, wg, wu, wd):
    n_tok, d = x.shape
    d_ff = wg.shape[1]
    row_spec = pl.BlockSpec((FFN_TILE, d), lambda i: (i, 0))
    return pl.pallas_call(
        _ffn_kernel,
        grid=(n_tok // FFN_TILE,),
        in_specs=[row_spec,
                  pl.BlockSpec((1, 6, d), lambda i: (i // tiles_per_mod, 0, 0)),
                  _const_spec(npre.shape), _const_spec(npost.shape),
                  _const_spec(wg.shape), _const_spec(wu.shape), _const_spec(wd.shape)],
        out_specs=row_spec,
        out_shape=jax.ShapeDtypeStruct((n_tok, d), F32),
        scratch_shapes=[pltpu.VMEM((FFN_TILE, d_ff), BF16)],
        compiler_params=pltpu.CompilerParams(
            dimension_semantics=("arbitrary",), vmem_limit_bytes=VMEM_LIMIT),
        name="ffn",
    )(x, mods, npre, npost, wg, wu, wd)


def kernel(x_prompt, x_sample, cache_k, cache_v, c, c_ctx, mod_w, mod_b, norm_pre_mix, norm_post_mix,
           norm_pre_ffn, norm_post_ffn, ab_w_in, sgu_w, sgu_b, sgu_g, ab_w_out, attn_w_qkv, attn_sink,
           attn_w_o, ffn_w_gate, ffn_w_up, ffn_w_down):
    bp, sp, d = x_prompt.shape
    bs, ss, _ = x_sample.shape
    depth = mod_w.shape[0]
    assert ROW_TILE % sp == 0 and ss == ROW_TILE and (bp * sp) % ROW_TILE == 0
    assert 1 + bs <= MOD_ROWS

    cond = jnp.zeros((MOD_ROWS, d), F32).at[0].set(c_ctx).at[1:1 + bs].set(c)
    mods = _mod_call(cond, mod_w, mod_b).reshape(depth, MOD_ROWS, 6, d)

    xp = x_prompt.reshape(bp * sp, d)
    xs = x_sample.reshape(bs * ss, d)
    state_k, state_v = [], []
    for layer in range(depth):
        mp = mods[layer, 0:1]
        ms = mods[layer, 1:1 + bs]
        npre = norm_pre_mix[layer][None, :]
        npost = norm_post_mix[layer][None, :]
        if layer % 2 == 0:
            e = layer // 2
            win = ab_w_in[e].astype(BF16)
            wout = ab_w_out[e].astype(BF16)
            sguw = sgu_w[e].astype(BF16)
            sgub_full = jnp.repeat(sgu_b[e].T, A_WIDTH // A_GROUPS, axis=1)
            sgug = sgu_g[e][None, :]
            xp = _mixer_ab_call(xp, mp, False, npre, npost, win, sguw, sgub_full, sgug, wout, sp)
            xs = _mixer_ab_call(xs, ms, True, npre, npost, win, sguw, sgub_full, sgug, wout, ss)
        else:
            o = layer // 2
            wqkv = attn_w_qkv[o].astype(BF16)
            wo = attn_w_o[o].astype(BF16)
            sink = attn_sink[o]
            xp, kp, vp = _attn_call(xp, mp, sink, npre, npost, wqkv, wo, sp)
            state_k.append(kp.reshape(bp, sp, N_KV_HEADS, HEAD_DIM))
            state_v.append(vp.reshape(bp, sp, N_KV_HEADS, HEAD_DIM))
            ck = cache_k[:, o].reshape(bs, -1, KV_WIDTH)
            cv = cache_v[:, o].reshape(bs, -1, KV_WIDTH)
            xs = _attn_call(xs, ms, sink, npre, npost, wqkv, wo, ss, cache=(ck, cv))
        npre = norm_pre_ffn[layer][None, :]
        npost = norm_post_ffn[layer][None, :]
        wg = ffn_w_gate[layer].astype(BF16)
        wu = ffn_w_up[layer].astype(BF16)
        wd = ffn_w_down[layer].astype(BF16)
        xp = _ffn_call(xp, mp, (bp * sp) // FFN_TILE, npre, npost, wg, wu, wd)
        xs = _ffn_call(xs, ms, ss // FFN_TILE, npre, npost, wg, wu, wd)
    return (xp.reshape(bp, sp, d), xs.reshape(bs, ss, d),
            jnp.stack(state_k, axis=1), jnp.stack(state_v, axis=1))
```

```python
import functools

import numpy as np
import jax
import jax.numpy as jnp
from jax import lax
from jax.experimental import pallas as pl
from jax.experimental.pallas import tpu as pltpu

F32 = jnp.float32
BF16 = jnp.bfloat16

D_MODEL = 1024
GRID_W = 64
CHUNK = 128
A_GROUPS = 4
A_WIDTH = D_MODEL // 2
B_GROUPS = 4
B_WIDTH = D_MODEL // 2
B_GROUP_DIM = B_WIDTH // B_GROUPS
HEAD_DIM = 128
N_HEADS = D_MODEL // HEAD_DIM
N_KV_HEADS = 2
KV_GROUP = N_HEADS // N_KV_HEADS
KV_WIDTH = N_KV_HEADS * HEAD_DIM
WINDOW = 128
Q_BLOCK = 128
AXIS_DIM = HEAD_DIM // 2
ROPE_BASE = 10000.0
EPS = 1e-6
NEG = -1e30
LOG2E = 1.4426950408889634

ROW_TILE = 1024
ROW_BLOCK = 256
FFN_TILE = 512
FFN_CHUNK = 768
MOD_ROWS = 16
MOD_COLS = 1536
VMEM_LIMIT = 56 * 1024 * 1024


def _dot(a, b):
    return jnp.dot(a, b, preferred_element_type=F32)


def _dot_t(a, b):
    return lax.dot_general(a, b, (((1,), (1,)), ((), ())), preferred_element_type=F32)


def _rms(x, g):
    ms = jnp.mean(x * x, axis=-1, keepdims=True)
    return x * lax.rsqrt(ms + EPS) * g


def _const_spec(shape):
    nd = len(shape)
    return pl.BlockSpec(shape, lambda i, _nd=nd: (0,) * _nd, pipeline_mode=pl.Buffered(1))


def _mod_kernel(cond_ref, w_ref, b_ref, o_ref):
    a = jax.nn.silu(cond_ref[...]).astype(BF16)
    o_ref[0] = _dot(a, w_ref[0].astype(BF16)) + b_ref[0]


def _mod_call(cond, mod_w, mod_b):
    depth, d, n = mod_w.shape
    return pl.pallas_call(
        _mod_kernel,
        grid=(depth, n // MOD_COLS),
        in_specs=[
            pl.BlockSpec((MOD_ROWS, d), lambda l, j: (0, 0)),
            pl.BlockSpec((1, d, MOD_COLS), lambda l, j: (l, 0, j)),
            pl.BlockSpec((1, 1, MOD_COLS), lambda l, j: (l, 0, j)),
        ],
        out_specs=pl.BlockSpec((1, MOD_ROWS, MOD_COLS), lambda l, j: (l, 0, j)),
        out_shape=jax.ShapeDtypeStruct((depth, MOD_ROWS, n), F32),
        compiler_params=pltpu.CompilerParams(
            dimension_semantics=("arbitrary", "arbitrary"), vmem_limit_bytes=VMEM_LIMIT),
        name="modulation",
    )(cond, mod_w, mod_b.reshape(depth, 1, n))


def _dft_tables(t):
    c = B_GROUP_DIM
    kc = (np.arange(c)[:, None] * np.arange(c)[None, :]) % c
    ac = 2.0 * np.pi * kc / c
    fc = np.concatenate([np.cos(ac), np.sin(ac)], axis=1)
    kt = (np.arange(t)[:, None] * np.arange(t)[None, :]) % t
    at = 2.0 * np.pi * kt / t
    ft = np.concatenate([np.cos(at), -np.sin(at)], axis=1) / np.sqrt(float(t * c))
    return jnp.asarray(fc, dtype=F32).astype(BF16), jnp.asarray(ft, dtype=F32).astype(BF16)


def _mixer_ab_kernel(x_ref, mod_ref, npre_ref, npost_ref, win_ref, sguw_ref, sgub_ref, sgug_ref,
                     fc_ref, ft_ref, wout_ref, o_ref,
                     u_scr, v_scr, zb_scr, ab_scr, zcs_scr, *, seq_len):
    rows_total = x_ref.shape[0]
    nseq = rows_total // seq_len
    shift, scale, gate = mod_ref[0, 0:1, :], mod_ref[0, 1:2, :], mod_ref[0, 2:3, :]

    for r in range(rows_total // ROW_BLOCK):
        rows = pl.ds(r * ROW_BLOCK, ROW_BLOCK)
        h = (_rms(x_ref[rows, :], npre_ref[...]) * (1.0 + scale) + shift).astype(BF16)
        z = _dot(h, win_ref[...])
        u_scr[rows, :] = jax.nn.gelu(z[:, :A_WIDTH])
        v = jax.nn.gelu(z[:, A_WIDTH:2 * A_WIDTH])
        vc = v - jnp.mean(v, axis=-1, keepdims=True)
        v = vc * lax.rsqrt(jnp.mean(vc * vc, axis=-1, keepdims=True) + EPS) * sgug_ref[...]
        v_scr[rows, :] = v.astype(BF16)
        zb_scr[rows, :] = z[:, 2 * A_WIDTH:].astype(BF16)

    gw = A_WIDTH // A_GROUPS
    for n in range(rows_total // CHUNK):
        rows = pl.ds(n * CHUNK, CHUNK)
        for g in range(A_GROUPS):
            lanes = slice(g * gw, (g + 1) * gw)
            mixed = _dot(sguw_ref[g], v_scr[rows, lanes]) + sgub_ref[:, lanes]
            ab_scr[rows, lanes] = (u_scr[rows, lanes] * mixed).astype(BF16)

    c = B_GROUP_DIM
    for s in range(nseq):
        rows = pl.ds(s * seq_len, seq_len)
        for g in range(B_GROUPS):
            lanes = slice(g * c, (g + 1) * c)
            zz = _dot(zb_scr[rows, lanes], fc_ref[...])
            zcs_scr[s, 0:seq_len, lanes] = zz[:, :c].astype(BF16)
            zcs_scr[s, seq_len:2 * seq_len, lanes] = zz[:, c:].astype(BF16)
        ab_scr[rows, A_WIDTH:] = _dot(ft_ref[...], zcs_scr[s]).astype(BF16)

    for r in range(rows_total // ROW_BLOCK):
        rows = pl.ds(r * ROW_BLOCK, ROW_BLOCK)
        y = _dot(ab_scr[rows, :], wout_ref[...])
        o_ref[rows, :] = x_ref[rows, :] + gate * _rms(y, npost_ref[...])


def _mixer_ab_call(x, mods, per_tile_mod, npre, npost, win, sguw, sgub_full, sgug, wout, seq_len):
    n_tok, d = x.shape
    fc, ft = _dft_tables(seq_len)
    nseq = ROW_TILE // seq_len
    mod_map = (lambda i: (i, 0, 0)) if per_tile_mod else (lambda i: (0, 0, 0))
    return pl.pallas_call(
        functools.partial(_mixer_ab_kernel, seq_len=seq_len),
        grid=(n_tok // ROW_TILE,),
        in_specs=[
            pl.BlockSpec((ROW_TILE, d), lambda i: (i, 0)),
            pl.BlockSpec((1, 6, d), mod_map),
            _const_spec(npre.shape), _const_spec(npost.shape), _const_spec(win.shape),
            _const_spec(sguw.shape), _const_spec(sgub_full.shape), _const_spec(sgug.shape),
            _const_spec(fc.shape), _const_spec(ft.shape), _const_spec(wout.shape),
        ],
        out_specs=pl.BlockSpec((ROW_TILE, d), lambda i: (i, 0)),
        out_shape=jax.ShapeDtypeStruct((n_tok, d), F32),
        scratch_shapes=[
            pltpu.VMEM((ROW_TILE, A_WIDTH), F32),
            pltpu.VMEM((ROW_TILE, A_WIDTH), BF16),
            pltpu.VMEM((ROW_TILE, B_WIDTH), BF16),
            pltpu.VMEM((ROW_TILE, A_WIDTH + B_WIDTH), BF16),
            pltpu.VMEM((nseq, 2 * seq_len, B_WIDTH), BF16),
        ],
        compiler_params=pltpu.CompilerParams(
            dimension_semantics=("arbitrary",), vmem_limit_bytes=VMEM_LIMIT),
        name="mixer_ab",
    )(x, mods, npre, npost, win, sguw, sgub_full, sgug, fc, ft, wout)


def _rope_tables(t):
    rows_n = t // GRID_W
    rows = jnp.repeat(jnp.arange(rows_n), GRID_W).astype(F32)
    cols = jnp.tile(jnp.arange(GRID_W), rows_n).astype(F32)
    inv = ROPE_BASE ** (-jnp.arange(0, AXIS_DIM, 2, dtype=F32) / AXIS_DIM)
    ar = rows[:, None] * inv
    ac = cols[:, None] * inv
    ang = jnp.concatenate([ar, ar, ac, ac], axis=-1)
    cos, sin = jnp.cos(ang), jnp.sin(ang)
    first_half = (jnp.arange(HEAD_DIM) % AXIS_DIM) < (AXIS_DIM // 2)
    sin_up = jnp.where(first_half[None, :], -sin, 0.0)
    sin_dn = jnp.where(first_half[None, :], 0.0, sin)
    return cos, sin_up, sin_dn


def _softmax_pv(score_parts, value_parts, sink_rows):
    rows = score_parts[0].shape[0]
    m = None
    for s in score_parts:
        sm = jnp.max(s, axis=-1, keepdims=True)
        m = sm if m is None else jnp.maximum(m, sm)
    mb = jnp.maximum(jnp.broadcast_to(m, (rows, HEAD_DIM)), sink_rows)
    acc = None
    for s, v in zip(score_parts, value_parts):
        p = jnp.concatenate(
            [jnp.exp2(s[:, i:i + HEAD_DIM] - mb) for i in range(0, s.shape[1], HEAD_DIM)], axis=1)
        o = _dot(p.astype(BF16), v)
        acc = o if acc is None else acc + o
    denom = acc[:, HEAD_DIM:] + jnp.exp2(sink_rows - mb)
    return acc[:, :HEAD_DIM] / denom


def _attn_kernel(*refs, seq_len, latent):
    if latent:
        (sink_ref, x_ref, mod_ref, npre_ref, npost_ref, wqkv_ref, wo_ref,
         cos_ref, sup_ref, sdn_ref, ck_ref, cv_ref, o_ref,
         q_scr, k_scr, va_scr, a_scr, cka_scr, cva_scr, bias_scr) = refs
    else:
        (sink_ref, x_ref, mod_ref, npre_ref, npost_ref, wqkv_ref, wo_ref,
         o_ref, ko_ref, vo_ref, q_scr, k_scr, va_scr, a_scr) = refs
    rows_total = x_ref.shape[0]
    shift, scale, gate = mod_ref[0, 0:1, :], mod_ref[0, 1:2, :], mod_ref[0, 2:3, :]
    qw = N_HEADS * HEAD_DIM
    q_scale = HEAD_DIM ** -0.5 * LOG2E
    ones = jnp.ones((ROW_BLOCK, HEAD_DIM), BF16)

    for r in range(rows_total // ROW_BLOCK):
        rows = pl.ds(r * ROW_BLOCK, ROW_BLOCK)
        h = (_rms(x_ref[rows, :], npre_ref[...]) * (1.0 + scale) + shift).astype(BF16)
        z = _dot(h, wqkv_ref[...])
        if latent:
            cos, sup, sdn = cos_ref[rows, :], sup_ref[rows, :], sdn_ref[rows, :]
            for hd in range(N_HEADS + N_KV_HEADS):
                zh = z[:, hd * HEAD_DIM:(hd + 1) * HEAD_DIM]
                zr = (zh * cos + pltpu.roll(zh, HEAD_DIM - AXIS_DIM // 2, axis=1) * sup
                      + pltpu.roll(zh, AXIS_DIM // 2, axis=1) * sdn)
                if hd < N_HEADS:
                    q_scr[rows, hd * HEAD_DIM:(hd + 1) * HEAD_DIM] = (zr * q_scale).astype(BF16)
                else:
                    k_scr[rows, (hd - N_HEADS) * HEAD_DIM:(hd - N_HEADS + 1) * HEAD_DIM] = zr.astype(BF16)
        else:
            q_scr[rows, :] = (z[:, :qw] * q_scale).astype(BF16)
            k_scr[rows, :] = z[:, qw:qw + KV_WIDTH].astype(BF16)
        for kh in range(N_KV_HEADS):
            kcols = slice(qw + kh * HEAD_DIM, qw + (kh + 1) * HEAD_DIM)
            vcols = slice(qw + KV_WIDTH + kh * HEAD_DIM, qw + KV_WIDTH + (kh + 1) * HEAD_DIM)
            va_scr[rows, 2 * kh * HEAD_DIM:(2 * kh + 1) * HEAD_DIM] = z[:, vcols].astype(BF16)
            va_scr[rows, (2 * kh + 1) * HEAD_DIM:(2 * kh + 2) * HEAD_DIM] = ones
            if not latent:
                state_rows = pl.ds(N_KV_HEADS * r * ROW_BLOCK + kh, ROW_BLOCK, stride=N_KV_HEADS)
                ko_ref[state_rows, :] = z[:, kcols]
                vo_ref[state_rows, :] = z[:, vcols]

    def sink_rows(kh, rows_per_head):
        return jnp.concatenate(
            [jnp.full((rows_per_head, HEAD_DIM), sink_ref[kh * KV_GROUP + g] * LOG2E, F32)
             for g in range(KV_GROUP)], axis=0)

    def stacked_queries(rows, kh):
        return jnp.concatenate(
            [q_scr[rows, (kh * KV_GROUP + g) * HEAD_DIM:(kh * KV_GROUP + g + 1) * HEAD_DIM]
             for g in range(KV_GROUP)], axis=0)

    def store_heads(rows, kh, out, rows_per_head):
        for g in range(KV_GROUP):
            hd = kh * KV_GROUP + g
            a_scr[rows, hd * HEAD_DIM:(hd + 1) * HEAD_DIM] = (
                out[g * rows_per_head:(g + 1) * rows_per_head].astype(BF16))

    if latent:
        nblk = seq_len // Q_BLOCK
        assert nblk >= 3 and WINDOW == Q_BLOCK
        past = ck_ref.shape[1] // N_KV_HEADS
        for kh in range(N_KV_HEADS):
            cache_rows = pl.ds(kh, past, stride=N_KV_HEADS)
            cka_scr[:, kh * HEAD_DIM:(kh + 1) * HEAD_DIM] = ck_ref[0, cache_rows, :].astype(BF16)
            cva_scr[:, 2 * kh * HEAD_DIM:(2 * kh + 1) * HEAD_DIM] = cv_ref[0, cache_rows, :].astype(BF16)
            cva_scr[:, (2 * kh + 1) * HEAD_DIM:(2 * kh + 2) * HEAD_DIM] = jnp.ones((past, HEAD_DIM), BF16)
        r_idx = lax.broadcasted_iota(jnp.int32, (KV_GROUP * Q_BLOCK, 3 * Q_BLOCK), 0) % Q_BLOCK
        c_idx = lax.broadcasted_iota(jnp.int32, (KV_GROUP * Q_BLOCK, 3 * Q_BLOCK), 1)
        bias_scr[0] = jnp.where(c_idx - r_idx <= WINDOW, 0.0, NEG)
        bias_scr[1] = jnp.where((c_idx >= r_idx) & (c_idx - r_idx <= 2 * WINDOW), 0.0, NEG)
        bias_scr[2] = jnp.where(c_idx >= r_idx, 0.0, NEG)
        for j in range(nblk):
            rows = pl.ds(j * Q_BLOCK, Q_BLOCK)
            lo, hi = max(j - 1, 0), min(j + 2, nblk)
            band = pl.ds(lo * Q_BLOCK, (hi - lo) * Q_BLOCK)
            nk = (hi - lo) * Q_BLOCK
            variant = 0 if j == 0 else (2 if j == nblk - 1 else 1)
            for kh in range(N_KV_HEADS):
                kl = slice(kh * HEAD_DIM, (kh + 1) * HEAD_DIM)
                vl = slice(2 * kh * HEAD_DIM, (2 * kh + 2) * HEAD_DIM)
                qh = stacked_queries(rows, kh)
                s_band = _dot_t(qh, k_scr[band, kl]) + bias_scr[variant, :, 0:nk]
                s_ctx = _dot_t(qh, cka_scr[:, kl])
                out = _softmax_pv([s_band, s_ctx], [va_scr[band, vl], cva_scr[:, vl]],
                                  sink_rows(kh, Q_BLOCK))
                store_heads(rows, kh, out, Q_BLOCK)
    else:
        for s in range(rows_total // seq_len):
            rows = pl.ds(s * seq_len, seq_len)
            for kh in range(N_KV_HEADS):
                kl = slice(kh * HEAD_DIM, (kh + 1) * HEAD_DIM)
                vl = slice(2 * kh * HEAD_DIM, (2 * kh + 2) * HEAD_DIM)
                qh = stacked_queries(rows, kh)
                out = _softmax_pv([_dot_t(qh, k_scr[rows, kl])], [va_scr[rows, vl]], sink_rows(kh, seq_len))
                store_heads(rows, kh, out, seq_len)

    for r in range(rows_total // ROW_BLOCK):
        rows = pl.ds(r * ROW_BLOCK, ROW_BLOCK)
        y = _dot(a_scr[rows, :], wo_ref[...])
        o_ref[rows, :] = x_ref[rows, :] + gate * _rms(y, npost_ref[...])


def _attn_call(x, mods, sink, npre, npost, wqkv, wo, seq_len, cache=None):
    n_tok, d = x.shape
    latent = cache is not None
    row_spec = pl.BlockSpec((ROW_TILE, d), lambda i: (i, 0))
    smem_spec = pl.BlockSpec(memory_space=pltpu.SMEM)
    in_specs = [smem_spec, row_spec,
                pl.BlockSpec((1, 6, d), (lambda i: (i, 0, 0)) if latent else (lambda i: (0, 0, 0))),
                _const_spec(npre.shape), _const_spec(npost.shape),
                _const_spec(wqkv.shape), _const_spec(wo.shape)]
    args = [sink, x, mods, npre, npost, wqkv, wo]
    scratch_shapes = [
        pltpu.VMEM((ROW_TILE, N_HEADS * HEAD_DIM), BF16),
        pltpu.VMEM((ROW_TILE, KV_WIDTH), BF16),
        pltpu.VMEM((ROW_TILE, 2 * KV_WIDTH), BF16),
        pltpu.VMEM((ROW_TILE, N_HEADS * HEAD_DIM), BF16),
    ]
    if latent:
        ck, cv = cache
        past = ck.shape[1] // N_KV_HEADS
        cos, sup, sdn = _rope_tables(seq_len)
        cache_spec = pl.BlockSpec((1,) + ck.shape[1:], lambda i: (i, 0, 0))
        in_specs += [_const_spec(cos.shape)] * 3 + [cache_spec, cache_spec]
        args += [cos, sup, sdn, ck, cv]
        out_specs = row_spec
        out_shape = jax.ShapeDtypeStruct((n_tok, d), F32)
        scratch_shapes += [
            pltpu.VMEM((past, KV_WIDTH), BF16),
            pltpu.VMEM((past, 2 * KV_WIDTH), BF16),
            pltpu.VMEM((3, KV_GROUP * Q_BLOCK, 3 * Q_BLOCK), F32),
        ]
    else:
        kv_spec = pl.BlockSpec((N_KV_HEADS * ROW_TILE, HEAD_DIM), lambda i: (i, 0))
        out_specs = [row_spec, kv_spec, kv_spec]
        out_shape = [jax.ShapeDtypeStruct((n_tok, d), F32),
                     jax.ShapeDtypeStruct((N_KV_HEADS * n_tok, HEAD_DIM), F32),
                     jax.ShapeDtypeStruct((N_KV_HEADS * n_tok, HEAD_DIM), F32)]
    return pl.pallas_call(
        functools.partial(_attn_kernel, seq_len=seq_len, latent=latent),
        grid=(n_tok // ROW_TILE,),
        in_specs=in_specs,
        out_specs=out_specs,
        out_shape=out_shape,
        scratch_shapes=scratch_shapes,
        compiler_params=pltpu.CompilerParams(
            dimension_semantics=("arbitrary",), vmem_limit_bytes=VMEM_LIMIT),
        name="attn_latent" if latent else "attn_context",
    )(*args)


def _ffn_kernel(x_ref, mod_ref, npre_ref, npost_ref, wg_ref, wu_ref, wd_ref, o_ref, a_scr):
    shift, scale, gate = mod_ref[0, 3:4, :], mod_ref[0, 4:5, :], mod_ref[0, 5:6, :]
    x = x_ref[...]
    h = (_rms(x, npre_ref[...]) * (1.0 + scale) + shift).astype(BF16)
    d_ff = wg_ref.shape[1]
    for c0 in range(0, d_ff, FFN_CHUNK):
        cols = slice(c0, min(c0 + FFN_CHUNK, d_ff))
        g = _dot(h, wg_ref[:, cols])
        u = _dot(h, wu_ref[:, cols])
        a_scr[:, cols] = (jax.nn.silu(g) * u).astype(BF16)
    y = _dot(a_scr[...], wd_ref[...])
    o_ref[...] = x + gate * _rms(y, npost_ref[...])


def _ffn_call(x, mods, tiles_per_mod, npre, npost, wg, wu, wd):
    n_tok, d = x.shape
    d_ff = wg.shape[1]
    row_spec = pl.BlockSpec((FFN_TILE, d), lambda i: (i, 0))
    return pl.pallas_call(
        _ffn_kernel,
        grid=(n_tok // FFN_TILE,),
        in_specs=[row_spec,
                  pl.BlockSpec((1, 6, d), lambda i: (i // tiles_per_mod, 0, 0)),
                  _const_spec(npre.shape), _const_spec(npost.shape),
                  _const_spec(wg.shape), _const_spec(wu.shape), _const_spec(wd.shape)],
        out_specs=row_spec,
        out_shape=jax.ShapeDtypeStruct((n_tok, d), F32),
        scratch_shapes=[pltpu.VMEM((FFN_TILE, d_ff), BF16)],
        compiler_params=pltpu.CompilerParams(
            dimension_semantics=("arbitrary",), vmem_limit_bytes=VMEM_LIMIT),
        name="ffn",
    )(x, mods, npre, npost, wg, wu, wd)


def kernel(x_prompt, x_sample, cache_k, cache_v, c, c_ctx, mod_w, mod_b, norm_pre_mix, norm_post_mix,
           norm_pre_ffn, norm_post_ffn, ab_w_in, sgu_w, sgu_b, sgu_g, ab_w_out, attn_w_qkv, attn_sink,
           attn_w_o, ffn_w_gate, ffn_w_up, ffn_w_down):
    bp, sp, d = x_prompt.shape
    bs, ss, _ = x_sample.shape
    depth = mod_w.shape[0]
    assert ROW_TILE % sp == 0 and ss == ROW_TILE and (bp * sp) % ROW_TILE == 0
    assert 1 + bs <= MOD_ROWS

    cond = jnp.zeros((MOD_ROWS, d), F32).at[0].set(c_ctx).at[1:1 + bs].set(c)
    mods = _mod_call(cond, mod_w, mod_b).reshape(depth, MOD_ROWS, 6, d)

    xp = x_prompt.reshape(bp * sp, d)
    xs = x_sample.reshape(bs * ss, d)
    state_k, state_v = [], []
    for layer in range(depth):
        mp = mods[layer, 0:1]
        ms = mods[layer, 1:1 + bs]
        npre = norm_pre_mix[layer][None, :]
        npost = norm_post_mix[layer][None, :]
        if layer % 2 == 0:
            e = layer // 2
            win = ab_w_in[e].astype(BF16)
            wout = ab_w_out[e].astype(BF16)
            sguw = sgu_w[e].astype(BF16)
            sgub_full = jnp.repeat(sgu_b[e].T, A_WIDTH // A_GROUPS, axis=1)
            sgug = sgu_g[e][None, :]
            xp = _mixer_ab_call(xp, mp, False, npre, npost, win, sguw, sgub_full, sgug, wout, sp)
            xs = _mixer_ab_call(xs, ms, True, npre, npost, win, sguw, sgub_full, sgug, wout, ss)
        else:
            o = layer // 2
            wqkv = attn_w_qkv[o].astype(BF16)
            wo = attn_w_o[o].astype(BF16)
            sink = attn_sink[o]
            xp, kp, vp = _attn_call(xp, mp, sink, npre, npost, wqkv, wo, sp)
            state_k.append(kp.reshape(bp, sp, N_KV_HEADS, HEAD_DIM))
            state_v.append(vp.reshape(bp, sp, N_KV_HEADS, HEAD_DIM))
            ck = cache_k[:, o].reshape(bs, -1, HEAD_DIM)
            cv = cache_v[:, o].reshape(bs, -1, HEAD_DIM)
            xs = _attn_call(xs, ms, sink, npre, npost, wqkv, wo, ss, cache=(ck, cv))
        npre = norm_pre_ffn[layer][None, :]
        npost = norm_post_ffn[layer][None, :]
        wg = ffn_w_gate[layer].astype(BF16)
        wu = ffn_w_up[layer].astype(BF16)
        wd = ffn_w_down[layer].astype(BF16)
        xp = _ffn_call(xp, mp, (bp * sp) // FFN_TILE, npre, npost, wg, wu, wd)
        xs = _ffn_call(xs, ms, ss // FFN_TILE, npre, npost, wg, wu, wd)
    return (xp.reshape(bp, sp, d), xs.reshape(bs, ss, d),
            jnp.stack(state_k, axis=1), jnp.stack(state_v, axis=1))
```

```python
import functools

import numpy as np
import jax
import jax.numpy as jnp
from jax import lax
from jax.experimental import pallas as pl
from jax.experimental.pallas import tpu as pltpu

F32 = jnp.float32
BF16 = jnp.bfloat16

D_MODEL = 1024
GRID_W = 64
CHUNK = 128
A_GROUPS = 4
A_WIDTH = D_MODEL // 2
B_GROUPS = 4
B_WIDTH = D_MODEL // 2
B_GROUP_DIM = B_WIDTH // B_GROUPS
HEAD_DIM = 128
N_HEADS = D_MODEL // HEAD_DIM
N_KV_HEADS = 2
KV_GROUP = N_HEADS // N_KV_HEADS
KV_WIDTH = N_KV_HEADS * HEAD_DIM
WINDOW = 128
Q_BLOCK = 128
AXIS_DIM = HEAD_DIM // 2
ROPE_BASE = 10000.0
EPS = 1e-6
NEG = -1e30
LOG2E = 1.4426950408889634

ROW_TILE = 1024
ROW_BLOCK = 256
FFN_TILE = 512
FFN_CHUNK = 768
MOD_ROWS = 16
MOD_COLS = 1536
WEIGHT_CHUNK = 256
STAGE_SLOTS = 3
VMEM_LIMIT = 56 * 1024 * 1024


def _dot(a, b):
    return jnp.dot(a, b, preferred_element_type=F32)


def _dot_t(a, b):
    return lax.dot_general(a, b, (((1,), (1,)), ((), ())), preferred_element_type=F32)


def _rms(x, g):
    ms = jnp.mean(x * x, axis=-1, keepdims=True)
    return x * lax.rsqrt(ms + EPS) * g


def _const_spec(shape):
    nd = len(shape)
    return pl.BlockSpec(shape, lambda i, _nd=nd: (0,) * _nd, pipeline_mode=pl.Buffered(1))


HBM_SPEC = pl.BlockSpec(memory_space=pl.ANY)


def _staging_scratch(by_cols):
    shape = (STAGE_SLOTS, D_MODEL, WEIGHT_CHUNK) if by_cols else (STAGE_SLOTS, WEIGHT_CHUNK, D_MODEL)
    return [pltpu.VMEM(shape, F32), pltpu.SemaphoreType.DMA((STAGE_SLOTS,))]


def _stage_weight(w_hbm, layer, dst, stage, sem):
    k, n = dst.shape
    by_cols = stage.shape[2] == WEIGHT_CHUNK
    assert stage.shape[1:] == ((k, WEIGHT_CHUNK) if by_cols else (WEIGHT_CHUNK, n))
    size = n if by_cols else k
    assert size % WEIGHT_CHUNK == 0
    n_chunks = size // WEIGHT_CHUNK

    def copy(i):
        span = pl.ds(i * WEIGHT_CHUNK, WEIGHT_CHUNK)
        src = w_hbm.at[layer, :, span] if by_cols else w_hbm.at[layer, span, :]
        return pltpu.make_async_copy(src, stage.at[i % STAGE_SLOTS], sem.at[i % STAGE_SLOTS])

    for i in range(min(STAGE_SLOTS - 1, n_chunks)):
        copy(i).start()
    for i in range(n_chunks):
        if i + STAGE_SLOTS - 1 < n_chunks:
            copy(i + STAGE_SLOTS - 1).start()
        copy(i).wait()
        lo, hi = i * WEIGHT_CHUNK, (i + 1) * WEIGHT_CHUNK
        if by_cols:
            dst[:, lo:hi] = stage[i % STAGE_SLOTS].astype(BF16)
        else:
            dst[lo:hi, :] = stage[i % STAGE_SLOTS].astype(BF16)


def _mod_kernel(cond_ref, w_ref, b_ref, o_ref):
    a = jax.nn.silu(cond_ref[...]).astype(BF16)
    o_ref[0] = _dot(a, w_ref[0].astype(BF16)) + b_ref[0]


def _mod_call(cond, mod_w, mod_b):
    depth, d, n = mod_w.shape
    return pl.pallas_call(
        _mod_kernel,
        grid=(depth, n // MOD_COLS),
        in_specs=[
            pl.BlockSpec((MOD_ROWS, d), lambda l, j: (0, 0)),
            pl.BlockSpec((1, d, MOD_COLS), lambda l, j: (l, 0, j)),
            pl.BlockSpec((1, 1, MOD_COLS), lambda l, j: (l, 0, j)),
        ],
        out_specs=pl.BlockSpec((1, MOD_ROWS, MOD_COLS), lambda l, j: (l, 0, j)),
        out_shape=jax.ShapeDtypeStruct((depth, MOD_ROWS, n), F32),
        compiler_params=pltpu.CompilerParams(
            dimension_semantics=("arbitrary", "arbitrary"), vmem_limit_bytes=VMEM_LIMIT),
        name="modulation",
    )(cond, mod_w, mod_b.reshape(depth, 1, n))


def _dft_tables(t):
    c = B_GROUP_DIM
    kc = (np.arange(c)[:, None] * np.arange(c)[None, :]) % c
    ac = 2.0 * np.pi * kc / c
    fc = np.concatenate([np.cos(ac), np.sin(ac)], axis=1)
    kt = (np.arange(t)[:, None] * np.arange(t)[None, :]) % t
    at = 2.0 * np.pi * kt / t
    ft = np.concatenate([np.cos(at), -np.sin(at)], axis=1) / np.sqrt(float(t * c))
    return jnp.asarray(fc, dtype=F32).astype(BF16), jnp.asarray(ft, dtype=F32).astype(BF16)


def _mixer_ab_kernel(x_ref, mod_ref, npre_ref, npost_ref, win_hbm, sguw_ref, sgub_ref, sgug_ref,
                     fc_ref, ft_ref, wout_hbm, o_ref,
                     win_ref, wout_ref, stage, sem, u_scr, v_scr, zb_scr, ab_scr, zcs_scr,
                     *, seq_len, layer, widx):
    @pl.when(pl.program_id(0) == 0)
    def _():
        _stage_weight(win_hbm, widx, win_ref, stage, sem)
        _stage_weight(wout_hbm, widx, wout_ref, stage, sem)

    rows_total = x_ref.shape[0]
    nseq = rows_total // seq_len
    shift, scale, gate = mod_ref[0, 0, 0:1, :], mod_ref[0, 0, 1:2, :], mod_ref[0, 0, 2:3, :]
    npre, npost = npre_ref[layer:layer + 1, :], npost_ref[layer:layer + 1, :]

    for r in range(rows_total // ROW_BLOCK):
        rows = pl.ds(r * ROW_BLOCK, ROW_BLOCK)
        h = (_rms(x_ref[rows, :], npre) * (1.0 + scale) + shift).astype(BF16)
        z = _dot(h, win_ref[...])
        u_scr[rows, :] = jax.nn.gelu(z[:, :A_WIDTH])
        v = jax.nn.gelu(z[:, A_WIDTH:2 * A_WIDTH])
        vc = v - jnp.mean(v, axis=-1, keepdims=True)
        v = vc * lax.rsqrt(jnp.mean(vc * vc, axis=-1, keepdims=True) + EPS) * sgug_ref[widx:widx + 1, :]
        v_scr[rows, :] = v.astype(BF16)
        zb_scr[rows, :] = z[:, 2 * A_WIDTH:].astype(BF16)

    gw = A_WIDTH // A_GROUPS
    for n in range(rows_total // CHUNK):
        rows = pl.ds(n * CHUNK, CHUNK)
        for g in range(A_GROUPS):
            lanes = slice(g * gw, (g + 1) * gw)
            mixed = _dot(sguw_ref[widx, g].astype(BF16), v_scr[rows, lanes]) + sgub_ref[:, lanes]
            ab_scr[rows, lanes] = (u_scr[rows, lanes] * mixed).astype(BF16)

    c = B_GROUP_DIM
    for s in range(nseq):
        rows = pl.ds(s * seq_len, seq_len)
        for g in range(B_GROUPS):
            lanes = slice(g * c, (g + 1) * c)
            zz = _dot(zb_scr[rows, lanes], fc_ref[...])
            zcs_scr[s, 0:seq_len, lanes] = zz[:, :c].astype(BF16)
            zcs_scr[s, seq_len:2 * seq_len, lanes] = zz[:, c:].astype(BF16)
        ab_scr[rows, A_WIDTH:] = _dot(ft_ref[...], zcs_scr[s]).astype(BF16)

    for r in range(rows_total // ROW_BLOCK):
        rows = pl.ds(r * ROW_BLOCK, ROW_BLOCK)
        y = _dot(ab_scr[rows, :], wout_ref[...])
        o_ref[rows, :] = x_ref[rows, :] + gate * _rms(y, npost)


def _mixer_ab_call(x, mods, per_tile_mod, npre, npost, win, sguw, sgub_full, sgug, wout, seq_len, layer, widx):
    n_tok, d = x.shape
    fc, ft = _dft_tables(seq_len)
    nseq = ROW_TILE // seq_len
    mod_map = (lambda i: (layer, 1 + i, 0, 0)) if per_tile_mod else (lambda i: (layer, 0, 0, 0))
    return pl.pallas_call(
        functools.partial(_mixer_ab_kernel, seq_len=seq_len, layer=layer, widx=widx),
        grid=(n_tok // ROW_TILE,),
        in_specs=[
            pl.BlockSpec((ROW_TILE, d), lambda i: (i, 0)),
            pl.BlockSpec((1, 1, 6, d), mod_map),
            _const_spec(npre.shape), _const_spec(npost.shape), HBM_SPEC,
            _const_spec(sguw.shape), _const_spec(sgub_full.shape), _const_spec(sgug.shape),
            _const_spec(fc.shape), _const_spec(ft.shape), HBM_SPEC,
        ],
        out_specs=pl.BlockSpec((ROW_TILE, d), lambda i: (i, 0)),
        out_shape=jax.ShapeDtypeStruct((n_tok, d), F32),
        scratch_shapes=[
            pltpu.VMEM(win.shape[1:], BF16),
            pltpu.VMEM(wout.shape[1:], BF16),
            *_staging_scratch(by_cols=True),
            pltpu.VMEM((ROW_TILE, A_WIDTH), F32),
            pltpu.VMEM((ROW_TILE, A_WIDTH), BF16),
            pltpu.VMEM((ROW_TILE, B_WIDTH), BF16),
            pltpu.VMEM((ROW_TILE, A_WIDTH + B_WIDTH), BF16),
            pltpu.VMEM((nseq, 2 * seq_len, B_WIDTH), BF16),
        ],
        compiler_params=pltpu.CompilerParams(
            dimension_semantics=("arbitrary",), vmem_limit_bytes=VMEM_LIMIT),
        name="mixer_ab",
    )(x, mods, npre, npost, win, sguw, sgub_full, sgug, fc, ft, wout)


def _rope_tables(t):
    rows_n = t // GRID_W
    rows = jnp.repeat(jnp.arange(rows_n), GRID_W).astype(F32)
    cols = jnp.tile(jnp.arange(GRID_W), rows_n).astype(F32)
    inv = ROPE_BASE ** (-jnp.arange(0, AXIS_DIM, 2, dtype=F32) / AXIS_DIM)
    ar = rows[:, None] * inv
    ac = cols[:, None] * inv
    ang = jnp.concatenate([ar, ar, ac, ac], axis=-1)
    cos, sin = jnp.cos(ang), jnp.sin(ang)
    first_half = (jnp.arange(HEAD_DIM) % AXIS_DIM) < (AXIS_DIM // 2)
    sin_up = jnp.where(first_half[None, :], -sin, 0.0)
    sin_dn = jnp.where(first_half[None, :], 0.0, sin)
    return cos, sin_up, sin_dn


def _softmax_pv(score_parts, value_parts, sink_rows):
    rows = score_parts[0].shape[0]
    m = None
    for s in score_parts:
        sm = jnp.max(s, axis=-1, keepdims=True)
        m = sm if m is None else jnp.maximum(m, sm)
    mb = jnp.maximum(jnp.broadcast_to(m, (rows, HEAD_DIM)), sink_rows)
    acc = None
    for s, v in zip(score_parts, value_parts):
        p = jnp.concatenate(
            [jnp.exp2(s[:, i:i + HEAD_DIM] - mb) for i in range(0, s.shape[1], HEAD_DIM)], axis=1)
        o = _dot(p.astype(BF16), v)
        acc = o if acc is None else acc + o
    denom = acc[:, HEAD_DIM:] + jnp.exp2(sink_rows - mb)
    return acc[:, :HEAD_DIM] / denom


def _attn_kernel(*refs, seq_len, latent, layer, widx):
    if latent:
        (sink_ref, x_ref, mod_ref, npre_ref, npost_ref, wqkv_hbm, wo_hbm,
         cos_ref, sup_ref, sdn_ref, ck_ref, cv_ref, o_ref,
         wqkv_ref, wo_ref, stage, sem, q_scr, k_scr, va_scr, a_scr, cka_scr, cva_scr, bias_scr) = refs
    else:
        (sink_ref, x_ref, mod_ref, npre_ref, npost_ref, wqkv_hbm, wo_hbm,
         o_ref, ko_ref, vo_ref, wqkv_ref, wo_ref, stage, sem, q_scr, k_scr, va_scr, a_scr) = refs

    @pl.when(pl.program_id(0) == 0)
    def _():
        _stage_weight(wqkv_hbm, widx, wqkv_ref, stage, sem)
        _stage_weight(wo_hbm, widx, wo_ref, stage, sem)

    rows_total = x_ref.shape[0]
    shift, scale, gate = mod_ref[0, 0, 0:1, :], mod_ref[0, 0, 1:2, :], mod_ref[0, 0, 2:3, :]
    npre, npost = npre_ref[layer:layer + 1, :], npost_ref[layer:layer + 1, :]
    qw = N_HEADS * HEAD_DIM
    q_scale = HEAD_DIM ** -0.5 * LOG2E
    ones = jnp.ones((ROW_BLOCK, HEAD_DIM), BF16)

    for r in range(rows_total // ROW_BLOCK):
        rows = pl.ds(r * ROW_BLOCK, ROW_BLOCK)
        h = (_rms(x_ref[rows, :], npre) * (1.0 + scale) + shift).astype(BF16)
        z = _dot(h, wqkv_ref[...])
        if latent:
            cos, sup, sdn = cos_ref[rows, :], sup_ref[rows, :], sdn_ref[rows, :]
            for hd in range(N_HEADS + N_KV_HEADS):
                zh = z[:, hd * HEAD_DIM:(hd + 1) * HEAD_DIM]
                zr = (zh * cos + pltpu.roll(zh, HEAD_DIM - AXIS_DIM // 2, axis=1) * sup
                      + pltpu.roll(zh, AXIS_DIM // 2, axis=1) * sdn)
                if hd < N_HEADS:
                    q_scr[rows, hd * HEAD_DIM:(hd + 1) * HEAD_DIM] = (zr * q_scale).astype(BF16)
                else:
                    k_scr[rows, (hd - N_HEADS) * HEAD_DIM:(hd - N_HEADS + 1) * HEAD_DIM] = zr.astype(BF16)
        else:
            q_scr[rows, :] = (z[:, :qw] * q_scale).astype(BF16)
            k_scr[rows, :] = z[:, qw:qw + KV_WIDTH].astype(BF16)
        for kh in range(N_KV_HEADS):
            kcols = slice(qw + kh * HEAD_DIM, qw + (kh + 1) * HEAD_DIM)
            vcols = slice(qw + KV_WIDTH + kh * HEAD_DIM, qw + KV_WIDTH + (kh + 1) * HEAD_DIM)
            va_scr[rows, 2 * kh * HEAD_DIM:(2 * kh + 1) * HEAD_DIM] = z[:, vcols].astype(BF16)
            va_scr[rows, (2 * kh + 1) * HEAD_DIM:(2 * kh + 2) * HEAD_DIM] = ones
            if not latent:
                state_rows = pl.ds(N_KV_HEADS * r * ROW_BLOCK + kh, ROW_BLOCK, stride=N_KV_HEADS)
                ko_ref[state_rows, :] = z[:, kcols]
                vo_ref[state_rows, :] = z[:, vcols]

    def sink_rows(kh, rows_per_head):
        return jnp.concatenate(
            [jnp.full((rows_per_head, HEAD_DIM), sink_ref[widx, kh * KV_GROUP + g] * LOG2E, F32)
             for g in range(KV_GROUP)], axis=0)

    def stacked_queries(rows, kh):
        return jnp.concatenate(
            [q_scr[rows, (kh * KV_GROUP + g) * HEAD_DIM:(kh * KV_GROUP + g + 1) * HEAD_DIM]
             for g in range(KV_GROUP)], axis=0)

    def store_heads(rows, kh, out, rows_per_head):
        for g in range(KV_GROUP):
            hd = kh * KV_GROUP + g
            a_scr[rows, hd * HEAD_DIM:(hd + 1) * HEAD_DIM] = (
                out[g * rows_per_head:(g + 1) * rows_per_head].astype(BF16))

    if latent:
        nblk = seq_len // Q_BLOCK
        assert nblk >= 3 and WINDOW == Q_BLOCK
        past = ck_ref.shape[1] // N_KV_HEADS
        for kh in range(N_KV_HEADS):
            cache_rows = pl.ds(kh, past, stride=N_KV_HEADS)
            cka_scr[:, kh * HEAD_DIM:(kh + 1) * HEAD_DIM] = ck_ref[0, cache_rows, :].astype(BF16)
            cva_scr[:, 2 * kh * HEAD_DIM:(2 * kh + 1) * HEAD_DIM] = cv_ref[0, cache_rows, :].astype(BF16)
            cva_scr[:, (2 * kh + 1) * HEAD_DIM:(2 * kh + 2) * HEAD_DIM] = jnp.ones((past, HEAD_DIM), BF16)
        r_idx = lax.broadcasted_iota(jnp.int32, (KV_GROUP * Q_BLOCK, 3 * Q_BLOCK), 0) % Q_BLOCK
        c_idx = lax.broadcasted_iota(jnp.int32, (KV_GROUP * Q_BLOCK, 3 * Q_BLOCK), 1)
        bias_scr[0] = jnp.where(c_idx - r_idx <= WINDOW, 0.0, NEG)
        bias_scr[1] = jnp.where((c_idx >= r_idx) & (c_idx - r_idx <= 2 * WINDOW), 0.0, NEG)
        bias_scr[2] = jnp.where(c_idx >= r_idx, 0.0, NEG)
        for j in range(nblk):
            rows = pl.ds(j * Q_BLOCK, Q_BLOCK)
            lo, hi = max(j - 1, 0), min(j + 2, nblk)
            band = pl.ds(lo * Q_BLOCK, (hi - lo) * Q_BLOCK)
            nk = (hi - lo) * Q_BLOCK
            variant = 0 if j == 0 else (2 if j == nblk - 1 else 1)
            for kh in range(N_KV_HEADS):
                kl = slice(kh * HEAD_DIM, (kh + 1) * HEAD_DIM)
                vl = slice(2 * kh * HEAD_DIM, (2 * kh + 2) * HEAD_DIM)
                qh = stacked_queries(rows, kh)
                s_band = _dot_t(qh, k_scr[band, kl]) + bias_scr[variant, :, 0:nk]
                s_ctx = _dot_t(qh, cka_scr[:, kl])
                out = _softmax_pv([s_band, s_ctx], [va_scr[band, vl], cva_scr[:, vl]],
                                  sink_rows(kh, Q_BLOCK))
                store_heads(rows, kh, out, Q_BLOCK)
    else:
        for s in range(rows_total // seq_len):
            rows = pl.ds(s * seq_len, seq_len)
            for kh in range(N_KV_HEADS):
                kl = slice(kh * HEAD_DIM, (kh + 1) * HEAD_DIM)
                vl = slice(2 * kh * HEAD_DIM, (2 * kh + 2) * HEAD_DIM)
                qh = stacked_queries(rows, kh)
                out = _softmax_pv([_dot_t(qh, k_scr[rows, kl])], [va_scr[rows, vl]], sink_rows(kh, seq_len))
                store_heads(rows, kh, out, seq_len)

    for r in range(rows_total // ROW_BLOCK):
        rows = pl.ds(r * ROW_BLOCK, ROW_BLOCK)
        y = _dot(a_scr[rows, :], wo_ref[...])
        o_ref[rows, :] = x_ref[rows, :] + gate * _rms(y, npost)


def _attn_call(x, mods, sink, npre, npost, wqkv, wo, seq_len, layer, widx, cache=None):
    n_tok, d = x.shape
    latent = cache is not None
    row_spec = pl.BlockSpec((ROW_TILE, d), lambda i: (i, 0))
    smem_spec = pl.BlockSpec(memory_space=pltpu.SMEM)
    in_specs = [smem_spec, row_spec,
                pl.BlockSpec((1, 1, 6, d),
                             (lambda i: (layer, 1 + i, 0, 0)) if latent else (lambda i: (layer, 0, 0, 0))),
                _const_spec(npre.shape), _const_spec(npost.shape), HBM_SPEC, HBM_SPEC]
    args = [sink, x, mods, npre, npost, wqkv, wo]
    scratch_shapes = [
        pltpu.VMEM(wqkv.shape[1:], BF16),
        pltpu.VMEM(wo.shape[1:], BF16),
        *_staging_scratch(by_cols=True),
        pltpu.VMEM((ROW_TILE, N_HEADS * HEAD_DIM), BF16),
        pltpu.VMEM((ROW_TILE, KV_WIDTH), BF16),
        pltpu.VMEM((ROW_TILE, 2 * KV_WIDTH), BF16),
        pltpu.VMEM((ROW_TILE, N_HEADS * HEAD_DIM), BF16),
    ]
    if latent:
        ck, cv = cache
        past = ck.shape[1] // N_KV_HEADS
        cos, sup, sdn = _rope_tables(seq_len)
        cache_spec = pl.BlockSpec((1,) + ck.shape[1:], lambda i: (i, 0, 0))
        in_specs += [_const_spec(cos.shape)] * 3 + [cache_spec, cache_spec]
        args += [cos, sup, sdn, ck, cv]
        out_specs = row_spec
        out_shape = jax.ShapeDtypeStruct((n_tok, d), F32)
        scratch_shapes += [
            pltpu.VMEM((past, KV_WIDTH), BF16),
            pltpu.VMEM((past, 2 * KV_WIDTH), BF16),
            pltpu.VMEM((3, KV_GROUP * Q_BLOCK, 3 * Q_BLOCK), F32),
        ]
    else:
        kv_spec = pl.BlockSpec((N_KV_HEADS * ROW_TILE, HEAD_DIM), lambda i: (i, 0))
        out_specs = [row_spec, kv_spec, kv_spec]
        out_shape = [jax.ShapeDtypeStruct((n_tok, d), F32),
                     jax.ShapeDtypeStruct((N_KV_HEADS * n_tok, HEAD_DIM), F32),
                     jax.ShapeDtypeStruct((N_KV_HEADS * n_tok, HEAD_DIM), F32)]
    return pl.pallas_call(
        functools.partial(_attn_kernel, seq_len=seq_len, latent=latent, layer=layer, widx=widx),
        grid=(n_tok // ROW_TILE,),
        in_specs=in_specs,
        out_specs=out_specs,
        out_shape=out_shape,
        scratch_shapes=scratch_shapes,
        compiler_params=pltpu.CompilerParams(
            dimension_semantics=("arbitrary",), vmem_limit_bytes=VMEM_LIMIT),
        name="attn_latent" if latent else "attn_context",
    )(*args)


def _ffn_kernel(xp_ref, xs_ref, mod_ref, npre_ref, npost_ref, wg_hbm, wu_hbm, wd_hbm, op_ref, os_ref,
                wg_ref, wu_ref, wd_ref, col_stage, col_sem, row_stage, row_sem, a_scr, *, layer, prompt_tiles):
    step = pl.program_id(0)

    @pl.when(step == 0)
    def _():
        _stage_weight(wg_hbm, layer, wg_ref, col_stage, col_sem)
        _stage_weight(wu_hbm, layer, wu_ref, col_stage, col_sem)
        _stage_weight(wd_hbm, layer, wd_ref, row_stage, row_sem)

    is_prompt = step < prompt_tiles
    shift, scale, gate = mod_ref[0, 0, 3:4, :], mod_ref[0, 0, 4:5, :], mod_ref[0, 0, 5:6, :]
    x = jnp.where(is_prompt, xp_ref[...], xs_ref[...])
    h = (_rms(x, npre_ref[layer:layer + 1, :]) * (1.0 + scale) + shift).astype(BF16)
    d_ff = wg_ref.shape[1]
    for c0 in range(0, d_ff, FFN_CHUNK):
        cols = slice(c0, min(c0 + FFN_CHUNK, d_ff))
        g = _dot(h, wg_ref[:, cols])
        u = _dot(h, wu_ref[:, cols])
        a_scr[:, cols] = (jax.nn.silu(g) * u).astype(BF16)
    y = _dot(a_scr[...], wd_ref[...])
    out = x + gate * _rms(y, npost_ref[layer:layer + 1, :])

    @pl.when(is_prompt)
    def _():
        op_ref[...] = out

    @pl.when(jnp.logical_not(is_prompt))
    def _():
        os_ref[...] = out


def _ffn_call(xp, xs, mods, npre, npost, wg, wu, wd, layer, latent_seq):
    (np_tok, d), (ns_tok, _) = xp.shape, xs.shape
    d_ff = wg.shape[2]
    tp, ts = np_tok // FFN_TILE, ns_tok // FFN_TILE
    tiles_per_seq = latent_seq // FFN_TILE
    p_spec = pl.BlockSpec((FFN_TILE, d), lambda i: (jnp.minimum(i, tp - 1), 0))
    s_spec = pl.BlockSpec((FFN_TILE, d), lambda i: (jnp.maximum(i - tp, 0), 0))
    mod_spec = pl.BlockSpec(
        (1, 1, 6, d), lambda i: (layer, jnp.maximum(i - tp + tiles_per_seq, 0) // tiles_per_seq, 0, 0))
    return pl.pallas_call(
        functools.partial(_ffn_kernel, layer=layer, prompt_tiles=tp),
        grid=(tp + ts,),
        in_specs=[p_spec, s_spec, mod_spec, _const_spec(npre.shape), _const_spec(npost.shape),
                  HBM_SPEC, HBM_SPEC, HBM_SPEC],
        out_specs=[p_spec, s_spec],
        out_shape=[jax.ShapeDtypeStruct((np_tok, d), F32), jax.ShapeDtypeStruct((ns_tok, d), F32)],
        scratch_shapes=[
            pltpu.VMEM((d, d_ff), BF16), pltpu.VMEM((d, d_ff), BF16), pltpu.VMEM((d_ff, d), BF16),
            *_staging_scratch(by_cols=True), *_staging_scratch(by_cols=False),
            pltpu.VMEM((FFN_TILE, d_ff), BF16),
        ],
        compiler_params=pltpu.CompilerParams(
            dimension_semantics=("arbitrary",), vmem_limit_bytes=VMEM_LIMIT),
        name="ffn",
    )(xp, xs, mods, npre, npost, wg, wu, wd)


def kernel(x_prompt, x_sample, cache_k, cache_v, c, c_ctx, mod_w, mod_b, norm_pre_mix, norm_post_mix,
           norm_pre_ffn, norm_post_ffn, ab_w_in, sgu_w, sgu_b, sgu_g, ab_w_out, attn_w_qkv, attn_sink,
           attn_w_o, ffn_w_gate, ffn_w_up, ffn_w_down):
    bp, sp, d = x_prompt.shape
    bs, ss, _ = x_sample.shape
    depth = mod_w.shape[0]
    assert ROW_TILE % sp == 0 and ss == ROW_TILE and (bp * sp) % ROW_TILE == 0
    assert 1 + bs <= MOD_ROWS

    cond = jnp.zeros((MOD_ROWS, d), F32).at[0].set(c_ctx).at[1:1 + bs].set(c)
    mods = _mod_call(cond, mod_w, mod_b).reshape(depth, MOD_ROWS, 6, d)

    xp = x_prompt.reshape(bp * sp, d)
    xs = x_sample.reshape(bs * ss, d)
    state_k, state_v = [], []
    for layer in range(depth):
        if layer % 2 == 0:
            e = layer // 2
            sgub_full = jnp.repeat(sgu_b[e].T, A_WIDTH // A_GROUPS, axis=1)
            args = (norm_pre_mix, norm_post_mix, ab_w_in, sgu_w, sgub_full, sgu_g, ab_w_out)
            xp = _mixer_ab_call(xp, mods, False, *args, sp, layer, e)
            xs = _mixer_ab_call(xs, mods, True, *args, ss, layer, e)
        else:
            o = layer // 2
            args = (attn_sink, norm_pre_mix, norm_post_mix, attn_w_qkv, attn_w_o)
            xp, kp, vp = _attn_call(xp, mods, *args, sp, layer, o)
            state_k.append(kp.reshape(bp, sp, N_KV_HEADS, HEAD_DIM))
            state_v.append(vp.reshape(bp, sp, N_KV_HEADS, HEAD_DIM))
            ck = cache_k[:, o].reshape(bs, -1, HEAD_DIM)
            cv = cache_v[:, o].reshape(bs, -1, HEAD_DIM)
            xs = _attn_call(xs, mods, *args, ss, layer, o, cache=(ck, cv))
        xp, xs = _ffn_call(xp, xs, mods, norm_pre_ffn, norm_post_ffn,
                           ffn_w_gate, ffn_w_up, ffn_w_down, layer, ss)
    return (xp.reshape(bp, sp, d), xs.reshape(bs, ss, d),
            jnp.stack(state_k, axis=1), jnp.stack(state_v, axis=1))
```

```python
import functools

import numpy as np
import jax
import jax.numpy as jnp
from jax import lax
from jax.experimental import pallas as pl
from jax.experimental.pallas import tpu as pltpu

F32 = jnp.float32
BF16 = jnp.bfloat16

D_MODEL = 1024
GRID_W = 64
CHUNK = 128
A_GROUPS = 4
A_WIDTH = D_MODEL // 2
B_GROUPS = 4
B_WIDTH = D_MODEL // 2
B_GROUP_DIM = B_WIDTH // B_GROUPS
HEAD_DIM = 128
N_HEADS = D_MODEL // HEAD_DIM
N_KV_HEADS = 2
KV_GROUP = N_HEADS // N_KV_HEADS
KV_WIDTH = N_KV_HEADS * HEAD_DIM
WINDOW = 128
Q_BLOCK = 128
AXIS_DIM = HEAD_DIM // 2
ROPE_BASE = 10000.0
EPS = 1e-6
NEG = -1e30
LOG2E = 1.4426950408889634

ROW_TILE = 1024
ROW_BLOCK = 512
FFN_BLOCK = 256
FFN_SPLIT_TILE = 512
FFN_CHUNK = 768
MOD_ROWS = 16
MOD_COLS = 1536
WEIGHT_CHUNK = 256
STAGE_SLOTS = 3
VMEM_LIMIT = 56 * 1024 * 1024


def _dot(a, b):
    return jnp.dot(a, b, preferred_element_type=F32)


def _dot_t(a, b):
    return lax.dot_general(a, b, (((1,), (1,)), ((), ())), preferred_element_type=F32)


def _modulated_norm(x, gain, shift):
    ms = jnp.mean(x * x, axis=-1, keepdims=True)
    return (x * lax.rsqrt(ms + EPS) * gain + shift).astype(BF16)


def _gated_residual(x, y, gain):
    ms = jnp.mean(y * y, axis=-1, keepdims=True)
    return x + y * lax.rsqrt(ms + EPS) * gain


def _const_spec(shape):
    nd = len(shape)
    return pl.BlockSpec(shape, lambda i, _nd=nd: (0,) * _nd)


HBM_SPEC = pl.BlockSpec(memory_space=pl.ANY)


def _staging_scratch(by_cols):
    shape = (STAGE_SLOTS, D_MODEL, WEIGHT_CHUNK) if by_cols else (STAGE_SLOTS, WEIGHT_CHUNK, D_MODEL)
    return [pltpu.VMEM(shape, F32), pltpu.SemaphoreType.DMA((STAGE_SLOTS,))]


def _stage_weight(w_hbm, layer, dst, stage, sem):
    k, n = dst.shape
    by_cols = stage.shape[2] == WEIGHT_CHUNK
    assert stage.shape[1:] == ((k, WEIGHT_CHUNK) if by_cols else (WEIGHT_CHUNK, n))
    size = n if by_cols else k
    assert size % WEIGHT_CHUNK == 0
    n_chunks = size // WEIGHT_CHUNK

    def copy(i):
        span = pl.ds(i * WEIGHT_CHUNK, WEIGHT_CHUNK)
        src = w_hbm.at[layer, :, span] if by_cols else w_hbm.at[layer, span, :]
        return pltpu.make_async_copy(src, stage.at[i % STAGE_SLOTS], sem.at[i % STAGE_SLOTS])

    for i in range(min(STAGE_SLOTS - 1, n_chunks)):
        copy(i).start()
    for i in range(n_chunks):
        if i + STAGE_SLOTS - 1 < n_chunks:
            copy(i + STAGE_SLOTS - 1).start()
        copy(i).wait()
        lo, hi = i * WEIGHT_CHUNK, (i + 1) * WEIGHT_CHUNK
        if by_cols:
            dst[:, lo:hi] = stage[i % STAGE_SLOTS].astype(BF16)
        else:
            dst[lo:hi, :] = stage[i % STAGE_SLOTS].astype(BF16)


def _stream_specs(x, prompt_tiles, d):
    tp = prompt_tiles
    if isinstance(x, tuple):
        return ([pl.BlockSpec((ROW_TILE, d), lambda i: (jnp.minimum(i, tp - 1), 0)),
                 pl.BlockSpec((ROW_TILE, d), lambda i: (jnp.maximum(i - tp, 0), 0))], list(x))
    return [pl.BlockSpec((ROW_TILE, d), lambda i: (i, 0))], [x]


def _mod_spec(layer, prompt_tiles, d, tiles_per_request=1):
    n = tiles_per_request
    return pl.BlockSpec((1, 1, 6, d), lambda i: (layer, jnp.maximum(i - prompt_tiles + n, 0) // n, 0, 0))


def _mod_kernel(cond_ref, w_ref, b_ref, o_ref):
    a = jax.nn.silu(cond_ref[...]).astype(BF16)
    o_ref[0] = _dot(a, w_ref[0].astype(BF16)) + b_ref[0]


def _mod_call(cond, mod_w, mod_b):
    depth, d, n = mod_w.shape
    return pl.pallas_call(
        _mod_kernel,
        grid=(depth, n // MOD_COLS),
        in_specs=[
            pl.BlockSpec((MOD_ROWS, d), lambda l, j: (0, 0)),
            pl.BlockSpec((1, d, MOD_COLS), lambda l, j: (l, 0, j)),
            pl.BlockSpec((1, 1, MOD_COLS), lambda l, j: (l, 0, j)),
        ],
        out_specs=pl.BlockSpec((1, MOD_ROWS, MOD_COLS), lambda l, j: (l, 0, j)),
        out_shape=jax.ShapeDtypeStruct((depth, MOD_ROWS, n), F32),
        compiler_params=pltpu.CompilerParams(
            dimension_semantics=("arbitrary", "arbitrary"), vmem_limit_bytes=VMEM_LIMIT),
        name="modulation",
    )(cond, mod_w, mod_b.reshape(depth, 1, n))


def _dft_tables(t):
    c = B_GROUP_DIM
    kc = (np.arange(c)[:, None] * np.arange(c)[None, :]) % c
    ac = 2.0 * np.pi * kc / c
    fc = np.concatenate([np.cos(ac), np.sin(ac)], axis=1)
    kt = (np.arange(t)[:, None] * np.arange(t)[None, :]) % t
    at = 2.0 * np.pi * kt / t
    ft = np.concatenate([np.cos(at), -np.sin(at)], axis=1) / np.sqrt(float(t * c))
    return jnp.asarray(fc, dtype=F32).astype(BF16), jnp.asarray(ft, dtype=F32).astype(BF16)


def _mixer_ab_tile(x_ref, o_ref, shift, pre_gain, post_gain, win_ref, sguw_ref, sgub_ref, sgug, fc_ref, ft_ref,
                   wout_ref, u_scr, v_scr, zb_scr, ab_scr, zcs_scr, seq_len):
    rows_total = x_ref.shape[0]

    for r in range(rows_total // ROW_BLOCK):
        rows = pl.ds(r * ROW_BLOCK, ROW_BLOCK)
        h = _modulated_norm(x_ref[rows, :], pre_gain, shift)
        z = _dot(h, win_ref[...])
        u_scr[rows, :] = jax.nn.gelu(z[:, :A_WIDTH])
        v = jax.nn.gelu(z[:, A_WIDTH:2 * A_WIDTH])
        vc = v - jnp.mean(v, axis=-1, keepdims=True)
        v = vc * lax.rsqrt(jnp.mean(vc * vc, axis=-1, keepdims=True) + EPS) * sgug
        v_scr[rows, :] = v.astype(BF16)
        zb_scr[rows, :] = z[:, 2 * A_WIDTH:].astype(BF16)

    gw = A_WIDTH // A_GROUPS
    for n in range(rows_total // CHUNK):
        rows = pl.ds(n * CHUNK, CHUNK)
        for g in range(A_GROUPS):
            lanes = slice(g * gw, (g + 1) * gw)
            mixed = _dot(sguw_ref[g].astype(BF16), v_scr[rows, lanes]) + sgub_ref[:, lanes]
            ab_scr[rows, lanes] = (u_scr[rows, lanes] * mixed).astype(BF16)

    c = B_GROUP_DIM
    for s in range(rows_total // seq_len):
        rows = pl.ds(s * seq_len, seq_len)
        cs_rows = pl.ds(2 * s * seq_len, 2 * seq_len)
        for g in range(B_GROUPS):
            lanes = slice(g * c, (g + 1) * c)
            zz = _dot(zb_scr[rows, lanes], fc_ref[...])
            zcs_scr[pl.ds(2 * s * seq_len, seq_len), lanes] = zz[:, :c].astype(BF16)
            zcs_scr[pl.ds((2 * s + 1) * seq_len, seq_len), lanes] = zz[:, c:].astype(BF16)
        ab_scr[rows, A_WIDTH:] = _dot(ft_ref[...], zcs_scr[cs_rows, :]).astype(BF16)

    for r in range(rows_total // ROW_BLOCK):
        rows = pl.ds(r * ROW_BLOCK, ROW_BLOCK)
        y = _dot(ab_scr[rows, :], wout_ref[...])
        o_ref[rows, :] = _gated_residual(x_ref[rows, :], y, post_gain)


def _mixer_ab_kernel(*refs, layer, widx, prompt_tiles, prompt_seq, latent_seq, n_x):
    x_refs, refs = refs[:n_x], refs[n_x:]
    (mod_ref, npre_ref, npost_ref, win_hbm, sguw_ref, sgub_ref, sgug_ref, fc_ref, ftp_ref, fts_ref, wout_hbm,
     o_ref, win_ref, wout_ref, stage, sem, u_scr, v_scr, zb_scr, ab_scr, zcs_scr) = refs
    step = pl.program_id(0)

    @pl.when(step == 0)
    def _():
        _stage_weight(win_hbm, widx, win_ref, stage, sem)
        _stage_weight(wout_hbm, widx, wout_ref, stage, sem)

    shift, scale, gate = mod_ref[0, 0, 0:1, :], mod_ref[0, 0, 1:2, :], mod_ref[0, 0, 2:3, :]
    pre_gain = npre_ref[layer:layer + 1, :] * (1.0 + scale)
    post_gain = npost_ref[layer:layer + 1, :] * gate
    common = (o_ref, shift, pre_gain, post_gain, win_ref, sguw_ref.at[widx], sgub_ref, sgug_ref[widx:widx + 1, :],
              fc_ref)
    scratch = (wout_ref, u_scr, v_scr, zb_scr, ab_scr, zcs_scr)

    @pl.when(step < prompt_tiles)
    def _():
        _mixer_ab_tile(x_refs[0], *common, ftp_ref, *scratch, prompt_seq)

    @pl.when(step >= prompt_tiles)
    def _():
        _mixer_ab_tile(x_refs[-1], *common, fts_ref, *scratch, latent_seq)


def _mixer_ab_call(x, n_prompt, n_latent, mods, npre, npost, win, sguw, sgub_full, sgug, wout,
                   prompt_seq, latent_seq, layer, widx):
    d = win.shape[1]
    tp, ts = n_prompt // ROW_TILE, n_latent // ROW_TILE
    fc, ftp = _dft_tables(prompt_seq)
    _, fts = _dft_tables(latent_seq)
    x_specs, x_args = _stream_specs(x, tp, d)
    return pl.pallas_call(
        functools.partial(_mixer_ab_kernel, layer=layer, widx=widx, prompt_tiles=tp,
                          prompt_seq=prompt_seq, latent_seq=latent_seq, n_x=len(x_args)),
        grid=(tp + ts,),
        in_specs=x_specs + [
            _mod_spec(layer, tp, d),
            _const_spec(npre.shape), _const_spec(npost.shape), HBM_SPEC,
            _const_spec(sguw.shape), _const_spec(sgub_full.shape), _const_spec(sgug.shape),
            _const_spec(fc.shape), _const_spec(ftp.shape), _const_spec(fts.shape), HBM_SPEC,
        ],
        out_specs=pl.BlockSpec((ROW_TILE, d), lambda i: (i, 0)),
        out_shape=jax.ShapeDtypeStruct((n_prompt + n_latent, d), F32),
        scratch_shapes=[
            pltpu.VMEM(win.shape[1:], BF16),
            pltpu.VMEM(wout.shape[1:], BF16),
            *_staging_scratch(by_cols=True),
            pltpu.VMEM((ROW_TILE, A_WIDTH), F32),
            pltpu.VMEM((ROW_TILE, A_WIDTH), BF16),
            pltpu.VMEM((ROW_TILE, B_WIDTH), BF16),
            pltpu.VMEM((ROW_TILE, A_WIDTH + B_WIDTH), BF16),
            pltpu.VMEM((2 * ROW_TILE, B_WIDTH), BF16),
        ],
        compiler_params=pltpu.CompilerParams(
            dimension_semantics=("arbitrary",), vmem_limit_bytes=VMEM_LIMIT),
        name="mixer_ab",
    )(*x_args, mods, npre, npost, win, sguw, sgub_full, sgug, fc, ftp, fts, wout)


def _rope_tables(t):
    rows_n = t // GRID_W
    rows = jnp.repeat(jnp.arange(rows_n), GRID_W).astype(F32)
    cols = jnp.tile(jnp.arange(GRID_W), rows_n).astype(F32)
    inv = ROPE_BASE ** (-jnp.arange(0, AXIS_DIM, 2, dtype=F32) / AXIS_DIM)
    ar = rows[:, None] * inv
    ac = cols[:, None] * inv
    ang = jnp.concatenate([ar, ar, ac, ac], axis=-1)
    cos, sin = jnp.cos(ang), jnp.sin(ang)
    first_half = (jnp.arange(HEAD_DIM) % AXIS_DIM) < (AXIS_DIM // 2)
    sin_up = jnp.where(first_half[None, :], -sin, 0.0)
    sin_dn = jnp.where(first_half[None, :], 0.0, sin)
    return cos, sin_up, sin_dn


def _softmax_pv(score_parts, value_parts, sink_rows):
    rows = score_parts[0].shape[0]
    m = None
    for s in score_parts:
        sm = jnp.max(s, axis=-1, keepdims=True)
        m = sm if m is None else jnp.maximum(m, sm)
    mb = jnp.maximum(jnp.broadcast_to(m, (rows, HEAD_DIM)), sink_rows)
    acc = None
    for s, v in zip(score_parts, value_parts):
        p = jnp.concatenate(
            [jnp.exp2(s[:, i:i + HEAD_DIM] - mb) for i in range(0, s.shape[1], HEAD_DIM)], axis=1)
        o = _dot(p.astype(BF16), v)
        acc = o if acc is None else acc + o
    denom = acc[:, HEAD_DIM:] + jnp.exp2(sink_rows - mb)
    return acc[:, :HEAD_DIM] / denom


def _attn_tile(x_ref, o_ref, shift, pre_gain, post_gain, sink, wqkv_ref, wo_ref, q_scr, k_scr, va_scr, a_scr,
               seq_len, state=None, latent=None):
    rows_total = x_ref.shape[0]
    qw = N_HEADS * HEAD_DIM
    q_scale = HEAD_DIM ** -0.5 * LOG2E
    ones = jnp.ones((ROW_BLOCK, HEAD_DIM), BF16)
    if latent is not None:
        cos_ref, sup_ref, sdn_ref, ck_ref, cv_ref, cka_scr, cva_scr, bias_scr = latent

    for r in range(rows_total // ROW_BLOCK):
        rows = pl.ds(r * ROW_BLOCK, ROW_BLOCK)
        h = _modulated_norm(x_ref[rows, :], pre_gain, shift)
        z = _dot(h, wqkv_ref[...])
        if latent is not None:
            cos, sup, sdn = cos_ref[rows, :], sup_ref[rows, :], sdn_ref[rows, :]
            for hd in range(N_HEADS + N_KV_HEADS):
                zh = z[:, hd * HEAD_DIM:(hd + 1) * HEAD_DIM]
                zr = (zh * cos + pltpu.roll(zh, HEAD_DIM - AXIS_DIM // 2, axis=1) * sup
                      + pltpu.roll(zh, AXIS_DIM // 2, axis=1) * sdn)
                if hd < N_HEADS:
                    q_scr[rows, hd * HEAD_DIM:(hd + 1) * HEAD_DIM] = (zr * q_scale).astype(BF16)
                else:
                    k_scr[rows, (hd - N_HEADS) * HEAD_DIM:(hd - N_HEADS + 1) * HEAD_DIM] = zr.astype(BF16)
        else:
            q_scr[rows, :] = (z[:, :qw] * q_scale).astype(BF16)
            k_scr[rows, :] = z[:, qw:qw + KV_WIDTH].astype(BF16)
        for kh in range(N_KV_HEADS):
            kcols = slice(qw + kh * HEAD_DIM, qw + (kh + 1) * HEAD_DIM)
            vcols = slice(qw + KV_WIDTH + kh * HEAD_DIM, qw + KV_WIDTH + (kh + 1) * HEAD_DIM)
            va_scr[rows, 2 * kh * HEAD_DIM:(2 * kh + 1) * HEAD_DIM] = z[:, vcols].astype(BF16)
            va_scr[rows, (2 * kh + 1) * HEAD_DIM:(2 * kh + 2) * HEAD_DIM] = ones
            if state is not None:
                state_rows = pl.ds(N_KV_HEADS * r * ROW_BLOCK + kh, ROW_BLOCK, stride=N_KV_HEADS)
                state[0][state_rows, :] = z[:, kcols]
                state[1][state_rows, :] = z[:, vcols]

    def sink_rows(kh, rows_per_head):
        return jnp.concatenate(
            [jnp.full((rows_per_head, HEAD_DIM), sink(kh * KV_GROUP + g) * LOG2E, F32)
             for g in range(KV_GROUP)], axis=0)

    def stacked_queries(rows, kh):
        return jnp.concatenate(
            [q_scr[rows, (kh * KV_GROUP + g) * HEAD_DIM:(kh * KV_GROUP + g + 1) * HEAD_DIM]
             for g in range(KV_GROUP)], axis=0)

    def store_heads(rows, kh, out, rows_per_head):
        for g in range(KV_GROUP):
            hd = kh * KV_GROUP + g
            a_scr[rows, hd * HEAD_DIM:(hd + 1) * HEAD_DIM] = (
                out[g * rows_per_head:(g + 1) * rows_per_head].astype(BF16))

    if latent is not None:
        nblk = seq_len // Q_BLOCK
        assert nblk >= 3 and WINDOW == Q_BLOCK and rows_total == seq_len
        past = ck_ref.shape[1] // N_KV_HEADS
        for kh in range(N_KV_HEADS):
            cache_rows = pl.ds(kh, past, stride=N_KV_HEADS)
            cka_scr[:, kh * HEAD_DIM:(kh + 1) * HEAD_DIM] = ck_ref[0, cache_rows, :].astype(BF16)
            cva_scr[:, 2 * kh * HEAD_DIM:(2 * kh + 1) * HEAD_DIM] = cv_ref[0, cache_rows, :].astype(BF16)
            cva_scr[:, (2 * kh + 1) * HEAD_DIM:(2 * kh + 2) * HEAD_DIM] = jnp.ones((past, HEAD_DIM), BF16)
        r_idx = lax.broadcasted_iota(jnp.int32, (KV_GROUP * Q_BLOCK, 3 * Q_BLOCK), 0) % Q_BLOCK
        c_idx = lax.broadcasted_iota(jnp.int32, (KV_GROUP * Q_BLOCK, 3 * Q_BLOCK), 1)
        bias_scr[0] = jnp.where(c_idx - r_idx <= WINDOW, 0.0, NEG)
        bias_scr[1] = jnp.where((c_idx >= r_idx) & (c_idx - r_idx <= 2 * WINDOW), 0.0, NEG)
        bias_scr[2] = jnp.where(c_idx >= r_idx, 0.0, NEG)
        for j in range(nblk):
            rows = pl.ds(j * Q_BLOCK, Q_BLOCK)
            lo, hi = max(j - 1, 0), min(j + 2, nblk)
            band = pl.ds(lo * Q_BLOCK, (hi - lo) * Q_BLOCK)
            nk = (hi - lo) * Q_BLOCK
            variant = 0 if j == 0 else (2 if j == nblk - 1 else 1)
            for kh in range(N_KV_HEADS):
                kl = slice(kh * HEAD_DIM, (kh + 1) * HEAD_DIM)
                vl = slice(2 * kh * HEAD_DIM, (2 * kh + 2) * HEAD_DIM)
                qh = stacked_queries(rows, kh)
                s_band = _dot_t(qh, k_scr[band, kl]) + bias_scr[variant, :, 0:nk]
                s_ctx = _dot_t(qh, cka_scr[:, kl])
                out = _softmax_pv([s_band, s_ctx], [va_scr[band, vl], cva_scr[:, vl]],
                                  sink_rows(kh, Q_BLOCK))
                store_heads(rows, kh, out, Q_BLOCK)
    else:
        for s in range(rows_total // seq_len):
            rows = pl.ds(s * seq_len, seq_len)
            for kh in range(N_KV_HEADS):
                kl = slice(kh * HEAD_DIM, (kh + 1) * HEAD_DIM)
                vl = slice(2 * kh * HEAD_DIM, (2 * kh + 2) * HEAD_DIM)
                qh = stacked_queries(rows, kh)
                out = _softmax_pv([_dot_t(qh, k_scr[rows, kl])], [va_scr[rows, vl]], sink_rows(kh, seq_len))
                store_heads(rows, kh, out, seq_len)

    for r in range(rows_total // ROW_BLOCK):
        rows = pl.ds(r * ROW_BLOCK, ROW_BLOCK)
        y = _dot(a_scr[rows, :], wo_ref[...])
        o_ref[rows, :] = _gated_residual(x_ref[rows, :], y, post_gain)


def _attn_kernel(*refs, layer, widx, prompt_tiles, prompt_seq, latent_seq, n_x):
    sink_ref, x_refs, refs = refs[0], refs[1:1 + n_x], refs[1 + n_x:]
    (mod_ref, npre_ref, npost_ref, wqkv_hbm, wo_hbm, cos_ref, sup_ref, sdn_ref, ck_ref, cv_ref,
     o_ref, ko_ref, vo_ref,
     wqkv_ref, wo_ref, stage, sem, q_scr, k_scr, va_scr, a_scr, cka_scr, cva_scr, bias_scr) = refs
    step = pl.program_id(0)

    @pl.when(step == 0)
    def _():
        _stage_weight(wqkv_hbm, widx, wqkv_ref, stage, sem)
        _stage_weight(wo_hbm, widx, wo_ref, stage, sem)

    shift, scale, gate = mod_ref[0, 0, 0:1, :], mod_ref[0, 0, 1:2, :], mod_ref[0, 0, 2:3, :]
    pre_gain = npre_ref[layer:layer + 1, :] * (1.0 + scale)
    post_gain = npost_ref[layer:layer + 1, :] * gate
    common = (o_ref, shift, pre_gain, post_gain, lambda h: sink_ref[widx, h],
              wqkv_ref, wo_ref, q_scr, k_scr, va_scr, a_scr)

    @pl.when(step < prompt_tiles)
    def _():
        _attn_tile(x_refs[0], *common, prompt_seq, state=(ko_ref, vo_ref))

    @pl.when(step >= prompt_tiles)
    def _():
        _attn_tile(x_refs[-1], *common, latent_seq,
                   latent=(cos_ref, sup_ref, sdn_ref, ck_ref, cv_ref, cka_scr, cva_scr, bias_scr))


def _attn_call(x, n_prompt, n_latent, mods, sink, npre, npost, wqkv, wo, cache_k, cache_v,
               prompt_seq, latent_seq, layer, widx):
    d = wqkv.shape[1]
    tp, ts = n_prompt // ROW_TILE, n_latent // ROW_TILE
    past = cache_k.shape[1] // N_KV_HEADS
    cos, sup, sdn = _rope_tables(latent_seq)
    x_specs, x_args = _stream_specs(x, tp, d)
    cache_spec = pl.BlockSpec((1,) + cache_k.shape[1:], lambda i: (jnp.maximum(i - tp, 0), 0, 0))
    state_spec = pl.BlockSpec((N_KV_HEADS * ROW_TILE, HEAD_DIM), lambda i: (jnp.minimum(i, tp - 1), 0))
    return pl.pallas_call(
        functools.partial(_attn_kernel, layer=layer, widx=widx, prompt_tiles=tp,
                          prompt_seq=prompt_seq, latent_seq=latent_seq, n_x=len(x_args)),
        grid=(tp + ts,),
        in_specs=[pl.BlockSpec(memory_space=pltpu.SMEM)] + x_specs + [
            _mod_spec(layer, tp, d),
            _const_spec(npre.shape), _const_spec(npost.shape), HBM_SPEC, HBM_SPEC,
            _const_spec(cos.shape), _const_spec(sup.shape), _const_spec(sdn.shape), cache_spec, cache_spec,
        ],
        out_specs=[pl.BlockSpec((ROW_TILE, d), lambda i: (i, 0)), state_spec, state_spec],
        out_shape=[jax.ShapeDtypeStruct((n_prompt + n_latent, d), F32),
                   jax.ShapeDtypeStruct((N_KV_HEADS * n_prompt, HEAD_DIM), F32),
                   jax.ShapeDtypeStruct((N_KV_HEADS * n_prompt, HEAD_DIM), F32)],
        scratch_shapes=[
            pltpu.VMEM(wqkv.shape[1:], BF16),
            pltpu.VMEM(wo.shape[1:], BF16),
            *_staging_scratch(by_cols=True),
            pltpu.VMEM((ROW_TILE, N_HEADS * HEAD_DIM), BF16),
            pltpu.VMEM((ROW_TILE, KV_WIDTH), BF16),
            pltpu.VMEM((ROW_TILE, 2 * KV_WIDTH), BF16),
            pltpu.VMEM((ROW_TILE, N_HEADS * HEAD_DIM), BF16),
            pltpu.VMEM((past, KV_WIDTH), BF16),
            pltpu.VMEM((past, 2 * KV_WIDTH), BF16),
            pltpu.VMEM((3, KV_GROUP * Q_BLOCK, 3 * Q_BLOCK), F32),
        ],
        compiler_params=pltpu.CompilerParams(
            dimension_semantics=("arbitrary",), vmem_limit_bytes=VMEM_LIMIT),
        name="attn",
    )(sink, *x_args, mods, npre, npost, wqkv, wo, cos, sup, sdn, cache_k, cache_v)


def _ffn_kernel(*refs, layer, prompt_tiles, split_out):
    if split_out:
        (x_ref, mod_ref, npre_ref, npost_ref, wg_hbm, wu_hbm, wd_hbm, op_ref, os_ref,
         wg_ref, wu_ref, wd_ref, col_stage, col_sem, row_stage, row_sem, a_scr) = refs
    else:
        (x_ref, mod_ref, npre_ref, npost_ref, wg_hbm, wu_hbm, wd_hbm, o_ref,
         wg_ref, wu_ref, wd_ref, col_stage, col_sem, row_stage, row_sem, a_scr) = refs
    step = pl.program_id(0)

    @pl.when(step == 0)
    def _():
        _stage_weight(wg_hbm, layer, wg_ref, col_stage, col_sem)
        _stage_weight(wu_hbm, layer, wu_ref, col_stage, col_sem)
        _stage_weight(wd_hbm, layer, wd_ref, row_stage, row_sem)

    shift, scale, gate = mod_ref[0, 0, 3:4, :], mod_ref[0, 0, 4:5, :], mod_ref[0, 0, 5:6, :]
    pre_gain = npre_ref[layer:layer + 1, :] * (1.0 + scale)
    post_gain = npost_ref[layer:layer + 1, :] * gate
    d_ff = wg_ref.shape[1]
    block = x_ref.shape[0] if split_out else FFN_BLOCK
    for b in range(x_ref.shape[0] // block):
        rows = pl.ds(b * block, block)
        h = _modulated_norm(x_ref[rows, :], pre_gain, shift)
        for c0 in range(0, d_ff, FFN_CHUNK):
            cols = slice(c0, min(c0 + FFN_CHUNK, d_ff))
            g = _dot(h, wg_ref[:, cols])
            u = _dot(h, wu_ref[:, cols])
            a_scr[rows, cols] = (jax.nn.silu(g) * u).astype(BF16)
        y = _dot(a_scr[rows, :], wd_ref[...])
        out = _gated_residual(x_ref[rows, :], y, post_gain)
        if not split_out:
            o_ref[rows, :] = out

    if split_out:
        @pl.when(step < prompt_tiles)
        def _():
            op_ref[...] = out

        @pl.when(step >= prompt_tiles)
        def _():
            os_ref[...] = out


def _ffn_call(x, n_prompt, n_latent, mods, npre, npost, wg, wu, wd, layer, latent_seq, split_out):
    d, d_ff = wg.shape[1], wg.shape[2]
    tile = FFN_SPLIT_TILE if split_out else ROW_TILE
    tp, ts = n_prompt // tile, n_latent // tile
    row_spec = pl.BlockSpec((tile, d), lambda i: (i, 0))
    if split_out:
        out_specs = [pl.BlockSpec((tile, d), lambda i: (jnp.minimum(i, tp - 1), 0)),
                     pl.BlockSpec((tile, d), lambda i: (jnp.maximum(i - tp, 0), 0))]
        out_shape = [jax.ShapeDtypeStruct((n_prompt, d), F32), jax.ShapeDtypeStruct((n_latent, d), F32)]
    else:
        out_specs = row_spec
        out_shape = jax.ShapeDtypeStruct((n_prompt + n_latent, d), F32)
    return pl.pallas_call(
        functools.partial(_ffn_kernel, layer=layer, prompt_tiles=tp, split_out=split_out),
        grid=(tp + ts,),
        in_specs=[row_spec, _mod_spec(layer, tp, d, latent_seq // tile),
                  _const_spec(npre.shape), _const_spec(npost.shape), HBM_SPEC, HBM_SPEC, HBM_SPEC],
        out_specs=out_specs,
        out_shape=out_shape,
        scratch_shapes=[
            pltpu.VMEM((d, d_ff), BF16), pltpu.VMEM((d, d_ff), BF16), pltpu.VMEM((d_ff, d), BF16),
            *_staging_scratch(by_cols=True), *_staging_scratch(by_cols=False),
            pltpu.VMEM((tile, d_ff), BF16),
        ],
        compiler_params=pltpu.CompilerParams(
            dimension_semantics=("arbitrary",), vmem_limit_bytes=VMEM_LIMIT),
        name="ffn",
    )(x, mods, npre, npost, wg, wu, wd)


def kernel(x_prompt, x_sample, cache_k, cache_v, c, c_ctx, mod_w, mod_b, norm_pre_mix, norm_post_mix,
           norm_pre_ffn, norm_post_ffn, ab_w_in, sgu_w, sgu_b, sgu_g, ab_w_out, attn_w_qkv, attn_sink,
           attn_w_o, ffn_w_gate, ffn_w_up, ffn_w_down):
    bp, sp, d = x_prompt.shape
    bs, ss, _ = x_sample.shape
    depth = mod_w.shape[0]
    n_prompt, n_latent = bp * sp, bs * ss
    assert d == D_MODEL and ROW_TILE % sp == 0 and ss == ROW_TILE and n_prompt % ROW_TILE == 0
    assert 1 + bs <= MOD_ROWS

    cond = jnp.zeros((MOD_ROWS, d), F32).at[0].set(c_ctx).at[1:1 + bs].set(c)
    mods = _mod_call(cond, mod_w, mod_b).reshape(depth, MOD_ROWS, 6, d)

    x = (x_prompt.reshape(n_prompt, d), x_sample.reshape(n_latent, d))
    state_k, state_v = [], []
    for layer in range(depth):
        if layer % 2 == 0:
            e = layer // 2
            sgub_full = jnp.repeat(sgu_b[e].T, A_WIDTH // A_GROUPS, axis=1)
            x = _mixer_ab_call(x, n_prompt, n_latent, mods, norm_pre_mix, norm_post_mix, ab_w_in, sgu_w,
                               sgub_full, sgu_g, ab_w_out, sp, ss, layer, e)
        else:
            o = layer // 2
            ck = cache_k[:, o].reshape(bs, -1, HEAD_DIM)
            cv = cache_v[:, o].reshape(bs, -1, HEAD_DIM)
            x, kp, vp = _attn_call(x, n_prompt, n_latent, mods, attn_sink, norm_pre_mix, norm_post_mix,
                                   attn_w_qkv, attn_w_o, ck, cv, sp, ss, layer, o)
            state_k.append(kp.reshape(bp, sp, N_KV_HEADS, HEAD_DIM))
            state_v.append(vp.reshape(bp, sp, N_KV_HEADS, HEAD_DIM))
        x = _ffn_call(x, n_prompt, n_latent, mods, norm_pre_ffn, norm_post_ffn,
                      ffn_w_gate, ffn_w_up, ffn_w_down, layer, ss, split_out=(layer == depth - 1))
    xp, xs = x
    return (xp.reshape(bp, sp, d), xs.reshape(bs, ss, d),
            jnp.stack(state_k, axis=1), jnp.stack(state_v, axis=1))
```

```python
import functools

import numpy as np
import jax
import jax.numpy as jnp
from jax import lax
from jax.experimental import pallas as pl
from jax.experimental.pallas import tpu as pltpu

F32 = jnp.float32
BF16 = jnp.bfloat16

D_MODEL = 1024
GRID_W = 64
CHUNK = 128
A_GROUPS = 4
A_WIDTH = D_MODEL // 2
B_GROUPS = 4
B_WIDTH = D_MODEL // 2
B_GROUP_DIM = B_WIDTH // B_GROUPS
HEAD_DIM = 128
N_HEADS = D_MODEL // HEAD_DIM
N_KV_HEADS = 2
KV_GROUP = N_HEADS // N_KV_HEADS
KV_WIDTH = N_KV_HEADS * HEAD_DIM
WINDOW = 128
Q_BLOCK = 128
AXIS_DIM = HEAD_DIM // 2
ROPE_BASE = 10000.0
EPS = 1e-6
NEG = -1e30
LOG2E = 1.4426950408889634

ROW_TILE = 1024
ROW_BLOCK = 512
FFN_BLOCK = 512
FFN_SPLIT_TILE = 512
FFN_CHUNK = 768
MOD_ROWS = 16
MOD_COLS = 1536
WEIGHT_CHUNK = 256
STAGE_SLOTS = 3
VMEM_LIMIT = 56 * 1024 * 1024


def _dot(a, b):
    return jnp.dot(a, b, preferred_element_type=F32)


def _dot_t(a, b):
    return lax.dot_general(a, b, (((1,), (1,)), ((), ())), preferred_element_type=F32)


def _modulated_norm(x, gain, shift):
    ms = jnp.mean(x * x, axis=-1, keepdims=True)
    return (x * lax.rsqrt(ms + EPS) * gain + shift).astype(BF16)


def _gated_residual(x, y, gain):
    ms = jnp.mean(y * y, axis=-1, keepdims=True)
    return x + y * lax.rsqrt(ms + EPS) * gain


def _const_spec(shape):
    nd = len(shape)
    return pl.BlockSpec(shape, lambda i, _nd=nd: (0,) * _nd)


HBM_SPEC = pl.BlockSpec(memory_space=pl.ANY)


def _staging_scratch(by_cols):
    shape = (STAGE_SLOTS, D_MODEL, WEIGHT_CHUNK) if by_cols else (STAGE_SLOTS, WEIGHT_CHUNK, D_MODEL)
    return [pltpu.VMEM(shape, F32), pltpu.SemaphoreType.DMA((STAGE_SLOTS,))]


class _WeightStager:
    def __init__(self, slabs, stage, sem):
        self.slabs, self.stage, self.sem = slabs, stage, sem
        self.by_cols = stage.shape[2] == WEIGHT_CHUNK
        self.started = self.done = 0

    def _copy(self, i):
        w_hbm, layer, _, index = self.slabs[i]
        span = pl.ds(index * WEIGHT_CHUNK, WEIGHT_CHUNK)
        src = w_hbm.at[layer, :, span] if self.by_cols else w_hbm.at[layer, span, :]
        return pltpu.make_async_copy(src, self.stage.at[i % STAGE_SLOTS], self.sem.at[i % STAGE_SLOTS])

    def _start_ahead(self):
        while self.started < min(self.done + STAGE_SLOTS, len(self.slabs)):
            self._copy(self.started).start()
            self.started += 1

    def advance(self, upto):
        self._start_ahead()
        while self.done < upto:
            i = self.done
            self._copy(i).wait()
            _, _, dst, index = self.slabs[i]
            lo, hi = index * WEIGHT_CHUNK, (index + 1) * WEIGHT_CHUNK
            if self.by_cols:
                dst[:, lo:hi] = self.stage[i % STAGE_SLOTS].astype(BF16)
            else:
                dst[lo:hi, :] = self.stage[i % STAGE_SLOTS].astype(BF16)
            self.done += 1
            self._start_ahead()

    def finish(self):
        self.advance(len(self.slabs))


def _weight_slabs(w_hbm, layer, dst, by_cols, first=0, last=None):
    size = dst.shape[1] if by_cols else dst.shape[0]
    assert size % WEIGHT_CHUNK == 0
    last = size // WEIGHT_CHUNK if last is None else last
    return [(w_hbm, layer, dst, i) for i in range(first, last)]


def _stage_weights(weights, stage, sem):
    by_cols = stage.shape[2] == WEIGHT_CHUNK
    slabs = [s for w_hbm, layer, dst in weights for s in _weight_slabs(w_hbm, layer, dst, by_cols)]
    _WeightStager(slabs, stage, sem).finish()


def _stream_specs(x, prompt_tiles, d):
    tp = prompt_tiles
    if isinstance(x, tuple):
        return ([pl.BlockSpec((ROW_TILE, d), lambda i: (jnp.minimum(i, tp - 1), 0)),
                 pl.BlockSpec((ROW_TILE, d), lambda i: (jnp.maximum(i - tp, 0), 0))], list(x))
    return [pl.BlockSpec((ROW_TILE, d), lambda i: (i, 0))], [x]


def _mod_spec(layer, prompt_tiles, d, tiles_per_request=1):
    n = tiles_per_request
    return pl.BlockSpec((1, 1, 6, d), lambda i: (layer, jnp.maximum(i - prompt_tiles + n, 0) // n, 0, 0))


def _mod_kernel(cond_ref, w_ref, b_ref, o_ref):
    a = jax.nn.silu(cond_ref[...]).astype(BF16)
    o_ref[0] = _dot(a, w_ref[0].astype(BF16)) + b_ref[0]


def _mod_call(cond, mod_w, mod_b):
    depth, d, n = mod_w.shape
    return pl.pallas_call(
        _mod_kernel,
        grid=(depth, n // MOD_COLS),
        in_specs=[
            pl.BlockSpec((MOD_ROWS, d), lambda l, j: (0, 0)),
            pl.BlockSpec((1, d, MOD_COLS), lambda l, j: (l, 0, j)),
            pl.BlockSpec((1, 1, MOD_COLS), lambda l, j: (l, 0, j)),
        ],
        out_specs=pl.BlockSpec((1, MOD_ROWS, MOD_COLS), lambda l, j: (l, 0, j)),
        out_shape=jax.ShapeDtypeStruct((depth, MOD_ROWS, n), F32),
        compiler_params=pltpu.CompilerParams(
            dimension_semantics=("arbitrary", "arbitrary"), vmem_limit_bytes=VMEM_LIMIT),
        name="modulation",
    )(cond, mod_w, mod_b.reshape(depth, 1, n))


def _dft_tables(t):
    c = B_GROUP_DIM
    kc = (np.arange(c)[:, None] * np.arange(c)[None, :]) % c
    ac = 2.0 * np.pi * kc / c
    fc = np.concatenate([np.cos(ac), np.sin(ac)], axis=1)
    kt = (np.arange(t)[:, None] * np.arange(t)[None, :]) % t
    at = 2.0 * np.pi * kt / t
    ft = np.concatenate([np.cos(at), -np.sin(at)], axis=1) / np.sqrt(float(t * c))
    return jnp.asarray(fc, dtype=F32).astype(BF16), jnp.asarray(ft, dtype=F32).astype(BF16)


def _mixer_ab_tile(x_ref, o_ref, shift, pre_gain, post_gain, win_ref, sguw_ref, sgub_ref, sgug, fc_ref, ft_ref,
                   wout_ref, u_scr, v_scr, zb_scr, ab_scr, zcs_scr, seq_len):
    rows_total = x_ref.shape[0]

    for r in range(rows_total // ROW_BLOCK):
        rows = pl.ds(r * ROW_BLOCK, ROW_BLOCK)
        h = _modulated_norm(x_ref[rows, :], pre_gain, shift)
        z = _dot(h, win_ref[...])
        u_scr[rows, :] = jax.nn.gelu(z[:, :A_WIDTH])
        v = jax.nn.gelu(z[:, A_WIDTH:2 * A_WIDTH])
        vc = v - jnp.mean(v, axis=-1, keepdims=True)
        v = vc * lax.rsqrt(jnp.mean(vc * vc, axis=-1, keepdims=True) + EPS) * sgug
        v_scr[rows, :] = v.astype(BF16)
        zb_scr[rows, :] = z[:, 2 * A_WIDTH:].astype(BF16)

    gw = A_WIDTH // A_GROUPS
    for n in range(rows_total // CHUNK):
        rows = pl.ds(n * CHUNK, CHUNK)
        for g in range(A_GROUPS):
            lanes = slice(g * gw, (g + 1) * gw)
            mixed = _dot(sguw_ref[g].astype(BF16), v_scr[rows, lanes]) + sgub_ref[:, lanes]
            ab_scr[rows, lanes] = (u_scr[rows, lanes] * mixed).astype(BF16)

    c = B_GROUP_DIM
    for s in range(rows_total // seq_len):
        rows = pl.ds(s * seq_len, seq_len)
        cs_rows = pl.ds(2 * s * seq_len, 2 * seq_len)
        for g in range(B_GROUPS):
            lanes = slice(g * c, (g + 1) * c)
            zz = _dot(zb_scr[rows, lanes], fc_ref[...])
            zcs_scr[pl.ds(2 * s * seq_len, seq_len), lanes] = zz[:, :c].astype(BF16)
            zcs_scr[pl.ds((2 * s + 1) * seq_len, seq_len), lanes] = zz[:, c:].astype(BF16)
        ab_scr[rows, A_WIDTH:] = _dot(ft_ref[...], zcs_scr[cs_rows, :]).astype(BF16)

    for r in range(rows_total // ROW_BLOCK):
        rows = pl.ds(r * ROW_BLOCK, ROW_BLOCK)
        y = _dot(ab_scr[rows, :], wout_ref[...])
        o_ref[rows, :] = _gated_residual(x_ref[rows, :], y, post_gain)


def _mixer_ab_kernel(*refs, layer, widx, prompt_tiles, prompt_seq, latent_seq, n_x):
    x_refs, refs = refs[:n_x], refs[n_x:]
    (mod_ref, npre_ref, npost_ref, win_hbm, sguw_ref, sgub_ref, sgug_ref, fc_ref, ftp_ref, fts_ref, wout_hbm,
     o_ref, win_ref, wout_ref, stage, sem, u_scr, v_scr, zb_scr, ab_scr, zcs_scr) = refs
    step = pl.program_id(0)

    @pl.when(step == 0)
    def _():
        _stage_weights([(win_hbm, widx, win_ref), (wout_hbm, widx, wout_ref)], stage, sem)

    shift, scale, gate = mod_ref[0, 0, 0:1, :], mod_ref[0, 0, 1:2, :], mod_ref[0, 0, 2:3, :]
    pre_gain = npre_ref[layer:layer + 1, :] * (1.0 + scale)
    post_gain = npost_ref[layer:layer + 1, :] * gate
    common = (o_ref, shift, pre_gain, post_gain, win_ref, sguw_ref.at[widx], sgub_ref, sgug_ref[widx:widx + 1, :],
              fc_ref)
    scratch = (wout_ref, u_scr, v_scr, zb_scr, ab_scr, zcs_scr)

    @pl.when(step < prompt_tiles)
    def _():
        _mixer_ab_tile(x_refs[0], *common, ftp_ref, *scratch, prompt_seq)

    @pl.when(step >= prompt_tiles)
    def _():
        _mixer_ab_tile(x_refs[-1], *common, fts_ref, *scratch, latent_seq)


def _mixer_ab_call(x, n_prompt, n_latent, mods, npre, npost, win, sguw, sgub_full, sgug, wout,
                   prompt_seq, latent_seq, layer, widx):
    d = win.shape[1]
    tp, ts = n_prompt // ROW_TILE, n_latent // ROW_TILE
    fc, ftp = _dft_tables(prompt_seq)
    _, fts = _dft_tables(latent_seq)
    x_specs, x_args = _stream_specs(x, tp, d)
    return pl.pallas_call(
        functools.partial(_mixer_ab_kernel, layer=layer, widx=widx, prompt_tiles=tp,
                          prompt_seq=prompt_seq, latent_seq=latent_seq, n_x=len(x_args)),
        grid=(tp + ts,),
        in_specs=x_specs + [
            _mod_spec(layer, tp, d),
            _const_spec(npre.shape), _const_spec(npost.shape), HBM_SPEC,
            _const_spec(sguw.shape), _const_spec(sgub_full.shape), _const_spec(sgug.shape),
            _const_spec(fc.shape), _const_spec(ftp.shape), _const_spec(fts.shape), HBM_SPEC,
        ],
        out_specs=pl.BlockSpec((ROW_TILE, d), lambda i: (i, 0)),
        out_shape=jax.ShapeDtypeStruct((n_prompt + n_latent, d), F32),
        scratch_shapes=[
            pltpu.VMEM(win.shape[1:], BF16),
            pltpu.VMEM(wout.shape[1:], BF16),
            *_staging_scratch(by_cols=True),
            pltpu.VMEM((ROW_TILE, A_WIDTH), F32),
            pltpu.VMEM((ROW_TILE, A_WIDTH), BF16),
            pltpu.VMEM((ROW_TILE, B_WIDTH), BF16),
            pltpu.VMEM((ROW_TILE, A_WIDTH + B_WIDTH), BF16),
            pltpu.VMEM((2 * ROW_TILE, B_WIDTH), BF16),
        ],
        compiler_params=pltpu.CompilerParams(
            dimension_semantics=("arbitrary",), vmem_limit_bytes=VMEM_LIMIT),
        name="mixer_ab",
    )(*x_args, mods, npre, npost, win, sguw, sgub_full, sgug, fc, ftp, fts, wout)


def _rope_tables(t):
    rows_n = t // GRID_W
    rows = jnp.repeat(jnp.arange(rows_n), GRID_W).astype(F32)
    cols = jnp.tile(jnp.arange(GRID_W), rows_n).astype(F32)
    inv = ROPE_BASE ** (-jnp.arange(0, AXIS_DIM, 2, dtype=F32) / AXIS_DIM)
    ar = rows[:, None] * inv
    ac = cols[:, None] * inv
    ang = jnp.concatenate([ar, ar, ac, ac], axis=-1)
    cos, sin = jnp.cos(ang), jnp.sin(ang)
    first_half = (jnp.arange(HEAD_DIM) % AXIS_DIM) < (AXIS_DIM // 2)
    sin_up = jnp.where(first_half[None, :], -sin, 0.0)
    sin_dn = jnp.where(first_half[None, :], 0.0, sin)
    return cos, sin_up, sin_dn


def _softmax_pv(score_parts, value_parts, sink_rows):
    rows = score_parts[0].shape[0]
    m = None
    for s in score_parts:
        sm = jnp.max(s, axis=-1, keepdims=True)
        m = sm if m is None else jnp.maximum(m, sm)
    mb = jnp.maximum(jnp.broadcast_to(m, (rows, HEAD_DIM)), sink_rows)
    acc = None
    for s, v in zip(score_parts, value_parts):
        p = jnp.concatenate(
            [jnp.exp2(s[:, i:i + HEAD_DIM] - mb) for i in range(0, s.shape[1], HEAD_DIM)], axis=1)
        o = _dot(p.astype(BF16), v)
        acc = o if acc is None else acc + o
    denom = acc[:, HEAD_DIM:] + jnp.exp2(sink_rows - mb)
    return acc[:, :HEAD_DIM] / denom


def _attn_tile(x_ref, o_ref, shift, pre_gain, post_gain, sink, wqkv_ref, wo_ref, q_scr, k_scr, va_scr, a_scr,
               seq_len, state=None, latent=None):
    rows_total = x_ref.shape[0]
    qw = N_HEADS * HEAD_DIM
    q_scale = HEAD_DIM ** -0.5 * LOG2E
    ones = jnp.ones((ROW_BLOCK, HEAD_DIM), BF16)
    if latent is not None:
        cos_ref, sup_ref, sdn_ref, ck_ref, cv_ref, cka_scr, cva_scr, bias_scr = latent

    for r in range(rows_total // ROW_BLOCK):
        rows = pl.ds(r * ROW_BLOCK, ROW_BLOCK)
        h = _modulated_norm(x_ref[rows, :], pre_gain, shift)
        z = _dot(h, wqkv_ref[...])
        if latent is not None:
            cos, sup, sdn = cos_ref[rows, :], sup_ref[rows, :], sdn_ref[rows, :]
            for hd in range(N_HEADS + N_KV_HEADS):
                zh = z[:, hd * HEAD_DIM:(hd + 1) * HEAD_DIM]
                zr = (zh * cos + pltpu.roll(zh, HEAD_DIM - AXIS_DIM // 2, axis=1) * sup
                      + pltpu.roll(zh, AXIS_DIM // 2, axis=1) * sdn)
                if hd < N_HEADS:
                    q_scr[rows, hd * HEAD_DIM:(hd + 1) * HEAD_DIM] = (zr * q_scale).astype(BF16)
                else:
                    k_scr[rows, (hd - N_HEADS) * HEAD_DIM:(hd - N_HEADS + 1) * HEAD_DIM] = zr.astype(BF16)
        else:
            q_scr[rows, :] = (z[:, :qw] * q_scale).astype(BF16)
            k_scr[rows, :] = z[:, qw:qw + KV_WIDTH].astype(BF16)
        for kh in range(N_KV_HEADS):
            kcols = slice(qw + kh * HEAD_DIM, qw + (kh + 1) * HEAD_DIM)
            vcols = slice(qw + KV_WIDTH + kh * HEAD_DIM, qw + KV_WIDTH + (kh + 1) * HEAD_DIM)
            va_scr[rows, 2 * kh * HEAD_DIM:(2 * kh + 1) * HEAD_DIM] = z[:, vcols].astype(BF16)
            va_scr[rows, (2 * kh + 1) * HEAD_DIM:(2 * kh + 2) * HEAD_DIM] = ones
            if state is not None:
                state_rows = pl.ds(N_KV_HEADS * r * ROW_BLOCK + kh, ROW_BLOCK, stride=N_KV_HEADS)
                state[0][state_rows, :] = z[:, kcols]
                state[1][state_rows, :] = z[:, vcols]

    def sink_rows(kh, rows_per_head):
        return jnp.concatenate(
            [jnp.full((rows_per_head, HEAD_DIM), sink(kh * KV_GROUP + g) * LOG2E, F32)
             for g in range(KV_GROUP)], axis=0)

    def stacked_queries(rows, kh):
        return jnp.concatenate(
            [q_scr[rows, (kh * KV_GROUP + g) * HEAD_DIM:(kh * KV_GROUP + g + 1) * HEAD_DIM]
             for g in range(KV_GROUP)], axis=0)

    def store_heads(rows, kh, out, rows_per_head):
        for g in range(KV_GROUP):
            hd = kh * KV_GROUP + g
            a_scr[rows, hd * HEAD_DIM:(hd + 1) * HEAD_DIM] = (
                out[g * rows_per_head:(g + 1) * rows_per_head].astype(BF16))

    if latent is not None:
        nblk = seq_len // Q_BLOCK
        assert nblk >= 3 and WINDOW == Q_BLOCK and rows_total == seq_len
        past = ck_ref.shape[1] // N_KV_HEADS
        for kh in range(N_KV_HEADS):
            cache_rows = pl.ds(kh, past, stride=N_KV_HEADS)
            cka_scr[:, kh * HEAD_DIM:(kh + 1) * HEAD_DIM] = ck_ref[0, cache_rows, :].astype(BF16)
            cva_scr[:, 2 * kh * HEAD_DIM:(2 * kh + 1) * HEAD_DIM] = cv_ref[0, cache_rows, :].astype(BF16)
            cva_scr[:, (2 * kh + 1) * HEAD_DIM:(2 * kh + 2) * HEAD_DIM] = jnp.ones((past, HEAD_DIM), BF16)
        r_idx = lax.broadcasted_iota(jnp.int32, (KV_GROUP * Q_BLOCK, 3 * Q_BLOCK), 0) % Q_BLOCK
        c_idx = lax.broadcasted_iota(jnp.int32, (KV_GROUP * Q_BLOCK, 3 * Q_BLOCK), 1)
        bias_scr[0] = jnp.where(c_idx - r_idx <= WINDOW, 0.0, NEG)
        bias_scr[1] = jnp.where((c_idx >= r_idx) & (c_idx - r_idx <= 2 * WINDOW), 0.0, NEG)
        bias_scr[2] = jnp.where(c_idx >= r_idx, 0.0, NEG)
        for j in range(nblk):
            rows = pl.ds(j * Q_BLOCK, Q_BLOCK)
            lo, hi = max(j - 1, 0), min(j + 2, nblk)
            band = pl.ds(lo * Q_BLOCK, (hi - lo) * Q_BLOCK)
            nk = (hi - lo) * Q_BLOCK
            variant = 0 if j == 0 else (2 if j == nblk - 1 else 1)
            for kh in range(N_KV_HEADS):
                kl = slice(kh * HEAD_DIM, (kh + 1) * HEAD_DIM)
                vl = slice(2 * kh * HEAD_DIM, (2 * kh + 2) * HEAD_DIM)
                qh = stacked_queries(rows, kh)
                s_band = _dot_t(qh, k_scr[band, kl]) + bias_scr[variant, :, 0:nk]
                s_ctx = _dot_t(qh, cka_scr[:, kl])
                out = _softmax_pv([s_band, s_ctx], [va_scr[band, vl], cva_scr[:, vl]],
                                  sink_rows(kh, Q_BLOCK))
                store_heads(rows, kh, out, Q_BLOCK)
    else:
        for s in range(rows_total // seq_len):
            rows = pl.ds(s * seq_len, seq_len)
            for kh in range(N_KV_HEADS):
                kl = slice(kh * HEAD_DIM, (kh + 1) * HEAD_DIM)
                vl = slice(2 * kh * HEAD_DIM, (2 * kh + 2) * HEAD_DIM)
                qh = stacked_queries(rows, kh)
                out = _softmax_pv([_dot_t(qh, k_scr[rows, kl])], [va_scr[rows, vl]], sink_rows(kh, seq_len))
                store_heads(rows, kh, out, seq_len)

    for r in range(rows_total // ROW_BLOCK):
        rows = pl.ds(r * ROW_BLOCK, ROW_BLOCK)
        y = _dot(a_scr[rows, :], wo_ref[...])
        o_ref[rows, :] = _gated_residual(x_ref[rows, :], y, post_gain)


def _attn_kernel(*refs, layer, widx, prompt_tiles, prompt_seq, latent_seq, n_x):
    sink_ref, x_refs, refs = refs[0], refs[1:1 + n_x], refs[1 + n_x:]
    (mod_ref, npre_ref, npost_ref, wqkv_hbm, wo_hbm, cos_ref, sup_ref, sdn_ref, ck_ref, cv_ref,
     o_ref, ko_ref, vo_ref,
     wqkv_ref, wo_ref, stage, sem, q_scr, k_scr, va_scr, a_scr, cka_scr, cva_scr, bias_scr) = refs
    step = pl.program_id(0)

    @pl.when(step == 0)
    def _():
        _stage_weights([(wqkv_hbm, widx, wqkv_ref), (wo_hbm, widx, wo_ref)], stage, sem)

    shift, scale, gate = mod_ref[0, 0, 0:1, :], mod_ref[0, 0, 1:2, :], mod_ref[0, 0, 2:3, :]
    pre_gain = npre_ref[layer:layer + 1, :] * (1.0 + scale)
    post_gain = npost_ref[layer:layer + 1, :] * gate
    common = (o_ref, shift, pre_gain, post_gain, lambda h: sink_ref[widx, h],
              wqkv_ref, wo_ref, q_scr, k_scr, va_scr, a_scr)

    @pl.when(step < prompt_tiles)
    def _():
        _attn_tile(x_refs[0], *common, prompt_seq, state=(ko_ref, vo_ref))

    @pl.when(step >= prompt_tiles)
    def _():
        _attn_tile(x_refs[-1], *common, latent_seq,
                   latent=(cos_ref, sup_ref, sdn_ref, ck_ref, cv_ref, cka_scr, cva_scr, bias_scr))


def _attn_call(x, n_prompt, n_latent, mods, sink, npre, npost, wqkv, wo, cache_k, cache_v,
               prompt_seq, latent_seq, layer, widx):
    d = wqkv.shape[1]
    tp, ts = n_prompt // ROW_TILE, n_latent // ROW_TILE
    past = cache_k.shape[1] // N_KV_HEADS
    cos, sup, sdn = _rope_tables(latent_seq)
    x_specs, x_args = _stream_specs(x, tp, d)
    cache_spec = pl.BlockSpec((1,) + cache_k.shape[1:], lambda i: (jnp.maximum(i - tp, 0), 0, 0))
    state_spec = pl.BlockSpec((N_KV_HEADS * ROW_TILE, HEAD_DIM), lambda i: (jnp.minimum(i, tp - 1), 0))
    return pl.pallas_call(
        functools.partial(_attn_kernel, layer=layer, widx=widx, prompt_tiles=tp,
                          prompt_seq=prompt_seq, latent_seq=latent_seq, n_x=len(x_args)),
        grid=(tp + ts,),
        in_specs=[pl.BlockSpec(memory_space=pltpu.SMEM)] + x_specs + [
            _mod_spec(layer, tp, d),
            _const_spec(npre.shape), _const_spec(npost.shape), HBM_SPEC, HBM_SPEC,
            _const_spec(cos.shape), _const_spec(sup.shape), _const_spec(sdn.shape), cache_spec, cache_spec,
        ],
        out_specs=[pl.BlockSpec((ROW_TILE, d), lambda i: (i, 0)), state_spec, state_spec],
        out_shape=[jax.ShapeDtypeStruct((n_prompt + n_latent, d), F32),
                   jax.ShapeDtypeStruct((N_KV_HEADS * n_prompt, HEAD_DIM), F32),
                   jax.ShapeDtypeStruct((N_KV_HEADS * n_prompt, HEAD_DIM), F32)],
        scratch_shapes=[
            pltpu.VMEM(wqkv.shape[1:], BF16),
            pltpu.VMEM(wo.shape[1:], BF16),
            *_staging_scratch(by_cols=True),
            pltpu.VMEM((ROW_TILE, N_HEADS * HEAD_DIM), BF16),
            pltpu.VMEM((ROW_TILE, KV_WIDTH), BF16),
            pltpu.VMEM((ROW_TILE, 2 * KV_WIDTH), BF16),
            pltpu.VMEM((ROW_TILE, N_HEADS * HEAD_DIM), BF16),
            pltpu.VMEM((past, KV_WIDTH), BF16),
            pltpu.VMEM((past, 2 * KV_WIDTH), BF16),
            pltpu.VMEM((3, KV_GROUP * Q_BLOCK, 3 * Q_BLOCK), F32),
        ],
        compiler_params=pltpu.CompilerParams(
            dimension_semantics=("arbitrary",), vmem_limit_bytes=VMEM_LIMIT),
        name="attn",
    )(sink, *x_args, mods, npre, npost, wqkv, wo, cos, sup, sdn, cache_k, cache_v)


def _ffn_kernel(*refs, layer, prompt_tiles, split_out):
    if split_out:
        (x_ref, mod_ref, npre_ref, npost_ref, wg_hbm, wu_hbm, wd_hbm, op_ref, os_ref,
         wg_ref, wu_ref, wd_ref, col_stage, col_sem, row_stage, row_sem, a_scr) = refs
    else:
        (x_ref, mod_ref, npre_ref, npost_ref, wg_hbm, wu_hbm, wd_hbm, o_ref,
         wg_ref, wu_ref, wd_ref, col_stage, col_sem, row_stage, row_sem, a_scr) = refs
    step = pl.program_id(0)
    shift, scale, gate = mod_ref[0, 0, 3:4, :], mod_ref[0, 0, 4:5, :], mod_ref[0, 0, 5:6, :]
    pre_gain = npre_ref[layer:layer + 1, :] * (1.0 + scale)
    post_gain = npost_ref[layer:layer + 1, :] * gate
    d_ff = wg_ref.shape[1]
    assert FFN_CHUNK % WEIGHT_CHUNK == 0 and prompt_tiles >= 1
    chunks = [(c0, min(c0 + FFN_CHUNK, d_ff)) for c0 in range(0, d_ff, FFN_CHUNK)]
    block = x_ref.shape[0] if split_out else FFN_BLOCK
    blocks = [pl.ds(b * block, block) for b in range(x_ref.shape[0] // block)]

    def gate_up(h, rows, c0, c1):
        g = _dot(h, wg_ref[:, c0:c1])
        u = _dot(h, wu_ref[:, c0:c1])
        a_scr[rows, c0:c1] = (jax.nn.silu(g) * u).astype(BF16)

    def down(rows):
        y = _dot(a_scr[rows, :], wd_ref[...])
        return _gated_residual(x_ref[rows, :], y, post_gain)

    @pl.when(step == 0)
    def _():
        col_slabs, col_marks = [], []
        for c0, c1 in chunks:
            span = (c0 // WEIGHT_CHUNK, c1 // WEIGHT_CHUNK)
            col_slabs += _weight_slabs(wg_hbm, layer, wg_ref, True, *span)
            col_slabs += _weight_slabs(wu_hbm, layer, wu_ref, True, *span)
            col_marks.append(len(col_slabs))
        col = _WeightStager(col_slabs, col_stage, col_sem)
        row = _WeightStager(_weight_slabs(wd_hbm, layer, wd_ref, False), row_stage, row_sem)
        col.advance(0)
        row.advance(0)
        hs = [_modulated_norm(x_ref[rows, :], pre_gain, shift) for rows in blocks]
        for ci, (c0, c1) in enumerate(chunks):
            col.advance(col_marks[ci])
            for h, rows in zip(hs, blocks):
                gate_up(h, rows, c0, c1)
            row.advance(len(row.slabs) * (ci + 1) // len(chunks))
        col.finish()
        row.finish()
        for rows in blocks:
            if split_out:
                op_ref[rows, :] = down(rows)
            else:
                o_ref[rows, :] = down(rows)

    @pl.when(step > 0)
    def _():
        for rows in blocks:
            h = _modulated_norm(x_ref[rows, :], pre_gain, shift)
            for c0, c1 in chunks:
                gate_up(h, rows, c0, c1)
            out = down(rows)
            if not split_out:
                o_ref[rows, :] = out
        if split_out:
            @pl.when(step < prompt_tiles)
            def _():
                op_ref[...] = out

            @pl.when(step >= prompt_tiles)
            def _():
                os_ref[...] = out


def _ffn_call(x, n_prompt, n_latent, mods, npre, npost, wg, wu, wd, layer, latent_seq, split_out):
    d, d_ff = wg.shape[1], wg.shape[2]
    tile = FFN_SPLIT_TILE if split_out else ROW_TILE
    tp, ts = n_prompt // tile, n_latent // tile
    row_spec = pl.BlockSpec((tile, d), lambda i: (i, 0))
    if split_out:
        out_specs = [pl.BlockSpec((tile, d), lambda i: (jnp.minimum(i, tp - 1), 0)),
                     pl.BlockSpec((tile, d), lambda i: (jnp.maximum(i - tp, 0), 0))]
        out_shape = [jax.ShapeDtypeStruct((n_prompt, d), F32), jax.ShapeDtypeStruct((n_latent, d), F32)]
    else:
        out_specs = row_spec
        out_shape = jax.ShapeDtypeStruct((n_prompt + n_latent, d), F32)
    return pl.pallas_call(
        functools.partial(_ffn_kernel, layer=layer, prompt_tiles=tp, split_out=split_out),
        grid=(tp + ts,),
        in_specs=[row_spec, _mod_spec(layer, tp, d, latent_seq // tile),
                  _const_spec(npre.shape), _const_spec(npost.shape), HBM_SPEC, HBM_SPEC, HBM_SPEC],
        out_specs=out_specs,
        out_shape=out_shape,
        scratch_shapes=[
            pltpu.VMEM((d, d_ff), BF16), pltpu.VMEM((d, d_ff), BF16), pltpu.VMEM((d_ff, d), BF16),
            *_staging_scratch(by_cols=True), *_staging_scratch(by_cols=False),
            pltpu.VMEM((tile, d_ff), BF16),
        ],
        compiler_params=pltpu.CompilerParams(
            dimension_semantics=("arbitrary",), vmem_limit_bytes=VMEM_LIMIT),
        name="ffn",
    )(x, mods, npre, npost, wg, wu, wd)


def kernel(x_prompt, x_sample, cache_k, cache_v, c, c_ctx, mod_w, mod_b, norm_pre_mix, norm_post_mix,
           norm_pre_ffn, norm_post_ffn, ab_w_in, sgu_w, sgu_b, sgu_g, ab_w_out, attn_w_qkv, attn_sink,
           attn_w_o, ffn_w_gate, ffn_w_up, ffn_w_down):
    bp, sp, d = x_prompt.shape
    bs, ss, _ = x_sample.shape
    depth = mod_w.shape[0]
    n_prompt, n_latent = bp * sp, bs * ss
    assert d == D_MODEL and ROW_TILE % sp == 0 and ss == ROW_TILE and n_prompt % ROW_TILE == 0
    assert 1 + bs <= MOD_ROWS

    cond = jnp.zeros((MOD_ROWS, d), F32).at[0].set(c_ctx).at[1:1 + bs].set(c)
    mods = _mod_call(cond, mod_w, mod_b).reshape(depth, MOD_ROWS, 6, d)

    x = (x_prompt.reshape(n_prompt, d), x_sample.reshape(n_latent, d))
    state_k, state_v = [], []
    for layer in range(depth):
        if layer % 2 == 0:
            e = layer // 2
            sgub_full = jnp.repeat(sgu_b[e].T, A_WIDTH // A_GROUPS, axis=1)
            x = _mixer_ab_call(x, n_prompt, n_latent, mods, norm_pre_mix, norm_post_mix, ab_w_in, sgu_w,
                               sgub_full, sgu_g, ab_w_out, sp, ss, layer, e)
        else:
            o = layer // 2
            ck = cache_k[:, o].reshape(bs, -1, HEAD_DIM)
            cv = cache_v[:, o].reshape(bs, -1, HEAD_DIM)
            x, kp, vp = _attn_call(x, n_prompt, n_latent, mods, attn_sink, norm_pre_mix, norm_post_mix,
                                   attn_w_qkv, attn_w_o, ck, cv, sp, ss, layer, o)
            state_k.append(kp.reshape(bp, sp, N_KV_HEADS, HEAD_DIM))
            state_v.append(vp.reshape(bp, sp, N_KV_HEADS, HEAD_DIM))
        x = _ffn_call(x, n_prompt, n_latent, mods, norm_pre_ffn, norm_post_ffn,
                      ffn_w_gate, ffn_w_up, ffn_w_down, layer, ss, split_out=(layer == depth - 1))
    xp, xs = x
    return (xp.reshape(bp, sp, d), xs.reshape(bs, ss, d),
            jnp.stack(state_k, axis=1), jnp.stack(state_v, axis=1))
```

```python
import functools

import numpy as np
import jax
import jax.numpy as jnp
from jax import lax
from jax.experimental import pallas as pl
from jax.experimental.pallas import tpu as pltpu

F32 = jnp.float32
BF16 = jnp.bfloat16

D_MODEL = 1024
GRID_W = 64
CHUNK = 128
A_GROUPS = 4
A_WIDTH = D_MODEL // 2
B_GROUPS = 4
B_WIDTH = D_MODEL // 2
B_GROUP_DIM = B_WIDTH // B_GROUPS
HEAD_DIM = 128
N_HEADS = D_MODEL // HEAD_DIM
N_KV_HEADS = 2
KV_GROUP = N_HEADS // N_KV_HEADS
KV_WIDTH = N_KV_HEADS * HEAD_DIM
WINDOW = 128
Q_BLOCK = 128
AXIS_DIM = HEAD_DIM // 2
ROPE_BASE = 10000.0
EPS = 1e-6
NEG = -1e30
LOG2E = 1.4426950408889634

ROW_TILE = 1024
ROW_BLOCK = 512
FFN_BLOCK = 512
FFN_SPLIT_TILE = 512
FFN_CHUNK = 768
MOD_ROWS = 16
MOD_COLS = 1536
WEIGHT_CHUNK = 128
STAGE_SLOTS = 8
VMEM_LIMIT = 56 * 1024 * 1024


def _dot(a, b):
    return jnp.dot(a, b, preferred_element_type=F32)


def _dot_t(a, b):
    return lax.dot_general(a, b, (((1,), (1,)), ((), ())), preferred_element_type=F32)


def _modulated_norm(x, gain, shift):
    ms = jnp.mean(x * x, axis=-1, keepdims=True)
    return (x * lax.rsqrt(ms + EPS) * gain + shift).astype(BF16)


def _gated_residual(x, y, gain):
    ms = jnp.mean(y * y, axis=-1, keepdims=True)
    return x + y * lax.rsqrt(ms + EPS) * gain


def _const_spec(shape):
    nd = len(shape)
    return pl.BlockSpec(shape, lambda i, _nd=nd: (0,) * _nd)


HBM_SPEC = pl.BlockSpec(memory_space=pl.ANY)


def _staging_scratch(by_cols):
    shape = (STAGE_SLOTS, D_MODEL, WEIGHT_CHUNK) if by_cols else (STAGE_SLOTS, WEIGHT_CHUNK, D_MODEL)
    return [pltpu.VMEM(shape, F32), pltpu.SemaphoreType.DMA((STAGE_SLOTS,))]


class _WeightStager:
    def __init__(self, slabs, stage, sem):
        self.slabs, self.stage, self.sem = slabs, stage, sem
        self.by_cols = stage.shape[2] == WEIGHT_CHUNK
        self.started = self.done = 0

    def _copy(self, i):
        w_hbm, layer, _, index = self.slabs[i]
        span = pl.ds(index * WEIGHT_CHUNK, WEIGHT_CHUNK)
        src = w_hbm.at[layer, :, span] if self.by_cols else w_hbm.at[layer, span, :]
        return pltpu.make_async_copy(src, self.stage.at[i % STAGE_SLOTS], self.sem.at[i % STAGE_SLOTS])

    def _start_ahead(self):
        while self.started < min(self.done + STAGE_SLOTS, len(self.slabs)):
            self._copy(self.started).start()
            self.started += 1

    def advance(self, upto):
        self._start_ahead()
        while self.done < upto:
            i = self.done
            self._copy(i).wait()
            _, _, dst, index = self.slabs[i]
            lo, hi = index * WEIGHT_CHUNK, (index + 1) * WEIGHT_CHUNK
            if self.by_cols:
                dst[:, lo:hi] = self.stage[i % STAGE_SLOTS].astype(BF16)
            else:
                dst[lo:hi, :] = self.stage[i % STAGE_SLOTS].astype(BF16)
            self.done += 1
            self._start_ahead()

    def finish(self):
        self.advance(len(self.slabs))


def _weight_slabs(w_hbm, layer, dst, by_cols, first=0, last=None):
    size = dst.shape[1] if by_cols else dst.shape[0]
    assert size % WEIGHT_CHUNK == 0
    last = size // WEIGHT_CHUNK if last is None else last
    return [(w_hbm, layer, dst, i) for i in range(first, last)]


def _stage_weights(weights, stage, sem):
    by_cols = stage.shape[2] == WEIGHT_CHUNK
    slabs = [s for w_hbm, layer, dst in weights for s in _weight_slabs(w_hbm, layer, dst, by_cols)]
    _WeightStager(slabs, stage, sem).finish()


def _stream_specs(x, prompt_tiles, d):
    tp = prompt_tiles
    if isinstance(x, tuple):
        return ([pl.BlockSpec((ROW_TILE, d), lambda i: (jnp.minimum(i, tp - 1), 0)),
                 pl.BlockSpec((ROW_TILE, d), lambda i: (jnp.maximum(i - tp, 0), 0))], list(x))
    return [pl.BlockSpec((ROW_TILE, d), lambda i: (i, 0))], [x]


def _mod_spec(layer, prompt_tiles, d, tiles_per_request=1):
    n = tiles_per_request
    return pl.BlockSpec((1, 1, 6, d), lambda i: (layer, jnp.maximum(i - prompt_tiles + n, 0) // n, 0, 0))


def _mod_kernel(cond_ref, w_ref, b_ref, o_ref):
    a = jax.nn.silu(cond_ref[...]).astype(BF16)
    o_ref[0] = _dot(a, w_ref[0].astype(BF16)) + b_ref[0]


def _mod_call(cond, mod_w, mod_b):
    depth, d, n = mod_w.shape
    return pl.pallas_call(
        _mod_kernel,
        grid=(depth, n // MOD_COLS),
        in_specs=[
            pl.BlockSpec((MOD_ROWS, d), lambda l, j: (0, 0)),
            pl.BlockSpec((1, d, MOD_COLS), lambda l, j: (l, 0, j)),
            pl.BlockSpec((1, 1, MOD_COLS), lambda l, j: (l, 0, j)),
        ],
        out_specs=pl.BlockSpec((1, MOD_ROWS, MOD_COLS), lambda l, j: (l, 0, j)),
        out_shape=jax.ShapeDtypeStruct((depth, MOD_ROWS, n), F32),
        compiler_params=pltpu.CompilerParams(
            dimension_semantics=("arbitrary", "arbitrary"), vmem_limit_bytes=VMEM_LIMIT),
        name="modulation",
    )(cond, mod_w, mod_b.reshape(depth, 1, n))


def _dft_tables(t):
    c = B_GROUP_DIM
    kc = (np.arange(c)[:, None] * np.arange(c)[None, :]) % c
    ac = 2.0 * np.pi * kc / c
    fc = np.concatenate([np.cos(ac), np.sin(ac)], axis=1)
    kt = (np.arange(t)[:, None] * np.arange(t)[None, :]) % t
    at = 2.0 * np.pi * kt / t
    ft = np.concatenate([np.cos(at), -np.sin(at)], axis=1) / np.sqrt(float(t * c))
    return jnp.asarray(fc, dtype=F32).astype(BF16), jnp.asarray(ft, dtype=F32).astype(BF16)


def _mixer_ab_tile(x_ref, o_ref, shift, pre_gain, post_gain, win_ref, sguw_ref, sgub_ref, sgug, fc_ref, ft_ref,
                   wout_ref, u_scr, v_scr, zb_scr, ab_scr, zcs_scr, seq_len):
    rows_total = x_ref.shape[0]

    for r in range(rows_total // ROW_BLOCK):
        rows = pl.ds(r * ROW_BLOCK, ROW_BLOCK)
        h = _modulated_norm(x_ref[rows, :], pre_gain, shift)
        z = _dot(h, win_ref[...])
        u_scr[rows, :] = jax.nn.gelu(z[:, :A_WIDTH])
        v = jax.nn.gelu(z[:, A_WIDTH:2 * A_WIDTH])
        vc = v - jnp.mean(v, axis=-1, keepdims=True)
        v = vc * lax.rsqrt(jnp.mean(vc * vc, axis=-1, keepdims=True) + EPS) * sgug
        v_scr[rows, :] = v.astype(BF16)
        zb_scr[rows, :] = z[:, 2 * A_WIDTH:].astype(BF16)

    gw = A_WIDTH // A_GROUPS
    for n in range(rows_total // CHUNK):
        rows = pl.ds(n * CHUNK, CHUNK)
        for g in range(A_GROUPS):
            lanes = slice(g * gw, (g + 1) * gw)
            mixed = _dot(sguw_ref[g].astype(BF16), v_scr[rows, lanes]) + sgub_ref[:, lanes]
            ab_scr[rows, lanes] = (u_scr[rows, lanes] * mixed).astype(BF16)

    c = B_GROUP_DIM
    for s in range(rows_total // seq_len):
        rows = pl.ds(s * seq_len, seq_len)
        cs_rows = pl.ds(2 * s * seq_len, 2 * seq_len)
        for g in range(B_GROUPS):
            lanes = slice(g * c, (g + 1) * c)
            zz = _dot(zb_scr[rows, lanes], fc_ref[...])
            zcs_scr[pl.ds(2 * s * seq_len, seq_len), lanes] = zz[:, :c].astype(BF16)
            zcs_scr[pl.ds((2 * s + 1) * seq_len, seq_len), lanes] = zz[:, c:].astype(BF16)
        ab_scr[rows, A_WIDTH:] = _dot(ft_ref[...], zcs_scr[cs_rows, :]).astype(BF16)

    for r in range(rows_total // ROW_BLOCK):
        rows = pl.ds(r * ROW_BLOCK, ROW_BLOCK)
        y = _dot(ab_scr[rows, :], wout_ref[...])
        o_ref[rows, :] = _gated_residual(x_ref[rows, :], y, post_gain)


def _mixer_ab_kernel(*refs, layer, widx, prompt_tiles, prompt_seq, latent_seq, n_x):
    x_refs, refs = refs[:n_x], refs[n_x:]
    (mod_ref, npre_ref, npost_ref, win_hbm, sguw_ref, sgub_ref, sgug_ref, fc_ref, ftp_ref, fts_ref, wout_hbm,
     o_ref, win_ref, wout_ref, stage, sem, u_scr, v_scr, zb_scr, ab_scr, zcs_scr) = refs
    step = pl.program_id(0)

    @pl.when(step == 0)
    def _():
        _stage_weights([(win_hbm, widx, win_ref), (wout_hbm, widx, wout_ref)], stage, sem)

    shift, scale, gate = mod_ref[0, 0, 0:1, :], mod_ref[0, 0, 1:2, :], mod_ref[0, 0, 2:3, :]
    pre_gain = npre_ref[layer:layer + 1, :] * (1.0 + scale)
    post_gain = npost_ref[layer:layer + 1, :] * gate
    common = (o_ref, shift, pre_gain, post_gain, win_ref, sguw_ref.at[widx], sgub_ref, sgug_ref[widx:widx + 1, :],
              fc_ref)
    scratch = (wout_ref, u_scr, v_scr, zb_scr, ab_scr, zcs_scr)

    @pl.when(step < prompt_tiles)
    def _():
        _mixer_ab_tile(x_refs[0], *common, ftp_ref, *scratch, prompt_seq)

    @pl.when(step >= prompt_tiles)
    def _():
        _mixer_ab_tile(x_refs[-1], *common, fts_ref, *scratch, latent_seq)


def _mixer_ab_call(x, n_prompt, n_latent, mods, npre, npost, win, sguw, sgub_full, sgug, wout,
                   prompt_seq, latent_seq, layer, widx):
    d = win.shape[1]
    tp, ts = n_prompt // ROW_TILE, n_latent // ROW_TILE
    fc, ftp = _dft_tables(prompt_seq)
    _, fts = _dft_tables(latent_seq)
    x_specs, x_args = _stream_specs(x, tp, d)
    return pl.pallas_call(
        functools.partial(_mixer_ab_kernel, layer=layer, widx=widx, prompt_tiles=tp,
                          prompt_seq=prompt_seq, latent_seq=latent_seq, n_x=len(x_args)),
        grid=(tp + ts,),
        in_specs=x_specs + [
            _mod_spec(layer, tp, d),
            _const_spec(npre.shape), _const_spec(npost.shape), HBM_SPEC,
            _const_spec(sguw.shape), _const_spec(sgub_full.shape), _const_spec(sgug.shape),
            _const_spec(fc.shape), _const_spec(ftp.shape), _const_spec(fts.shape), HBM_SPEC,
        ],
        out_specs=pl.BlockSpec((ROW_TILE, d), lambda i: (i, 0)),
        out_shape=jax.ShapeDtypeStruct((n_prompt + n_latent, d), F32),
        scratch_shapes=[
            pltpu.VMEM(win.shape[1:], BF16),
            pltpu.VMEM(wout.shape[1:], BF16),
            *_staging_scratch(by_cols=True),
            pltpu.VMEM((ROW_TILE, A_WIDTH), F32),
            pltpu.VMEM((ROW_TILE, A_WIDTH), BF16),
            pltpu.VMEM((ROW_TILE, B_WIDTH), BF16),
            pltpu.VMEM((ROW_TILE, A_WIDTH + B_WIDTH), BF16),
            pltpu.VMEM((2 * ROW_TILE, B_WIDTH), BF16),
        ],
        compiler_params=pltpu.CompilerParams(
            dimension_semantics=("arbitrary",), vmem_limit_bytes=VMEM_LIMIT),
        name="mixer_ab",
    )(*x_args, mods, npre, npost, win, sguw, sgub_full, sgug, fc, ftp, fts, wout)


def _rope_tables(t):
    rows_n = t // GRID_W
    rows = jnp.repeat(jnp.arange(rows_n), GRID_W).astype(F32)
    cols = jnp.tile(jnp.arange(GRID_W), rows_n).astype(F32)
    inv = ROPE_BASE ** (-jnp.arange(0, AXIS_DIM, 2, dtype=F32) / AXIS_DIM)
    ar = rows[:, None] * inv
    ac = cols[:, None] * inv
    ang = jnp.concatenate([ar, ar, ac, ac], axis=-1)
    cos, sin = jnp.cos(ang), jnp.sin(ang)
    first_half = (jnp.arange(HEAD_DIM) % AXIS_DIM) < (AXIS_DIM // 2)
    sin_up = jnp.where(first_half[None, :], -sin, 0.0)
    sin_dn = jnp.where(first_half[None, :], 0.0, sin)
    return cos, sin_up, sin_dn


def _softmax_pv(score_parts, value_parts, sink_rows):
    rows = score_parts[0].shape[0]
    folded = None
    for s in score_parts:
        for i in range(0, s.shape[1], HEAD_DIM):
            blk = s[:, i:i + HEAD_DIM]
            folded = blk if folded is None else jnp.maximum(folded, blk)
    m = jnp.max(folded, axis=-1, keepdims=True)
    mb = jnp.maximum(jnp.broadcast_to(m, (rows, HEAD_DIM)), sink_rows)
    acc = None
    for s, v in zip(score_parts, value_parts):
        p = jnp.concatenate(
            [jnp.exp2(s[:, i:i + HEAD_DIM] - mb) for i in range(0, s.shape[1], HEAD_DIM)], axis=1)
        o = _dot(p.astype(BF16), v)
        acc = o if acc is None else acc + o
    denom = acc[:, HEAD_DIM:] + jnp.exp2(sink_rows - mb)
    return acc[:, :HEAD_DIM] / denom


def _attn_tile(x_ref, o_ref, shift, pre_gain, post_gain, sink, wqkv_ref, wo_ref, q_scr, k_scr, va_scr, a_scr,
               seq_len, state=None, latent=None):
    rows_total = x_ref.shape[0]
    qw = N_HEADS * HEAD_DIM
    q_scale = HEAD_DIM ** -0.5 * LOG2E
    ones = jnp.ones((ROW_BLOCK, HEAD_DIM), BF16)
    if latent is not None:
        cos_ref, sup_ref, sdn_ref, ck_ref, cv_ref, cka_scr, cva_scr, bias_scr = latent

    for r in range(rows_total // ROW_BLOCK):
        rows = pl.ds(r * ROW_BLOCK, ROW_BLOCK)
        h = _modulated_norm(x_ref[rows, :], pre_gain, shift)
        z = _dot(h, wqkv_ref[...])
        if latent is not None:
            cos, sup, sdn = cos_ref[rows, :], sup_ref[rows, :], sdn_ref[rows, :]
            for hd in range(N_HEADS + N_KV_HEADS):
                zh = z[:, hd * HEAD_DIM:(hd + 1) * HEAD_DIM]
                zr = (zh * cos + pltpu.roll(zh, HEAD_DIM - AXIS_DIM // 2, axis=1) * sup
                      + pltpu.roll(zh, AXIS_DIM // 2, axis=1) * sdn)
                if hd < N_HEADS:
                    q_scr[rows, hd * HEAD_DIM:(hd + 1) * HEAD_DIM] = (zr * q_scale).astype(BF16)
                else:
                    k_scr[rows, (hd - N_HEADS) * HEAD_DIM:(hd - N_HEADS + 1) * HEAD_DIM] = zr.astype(BF16)
        else:
            q_scr[rows, :] = (z[:, :qw] * q_scale).astype(BF16)
            k_scr[rows, :] = z[:, qw:qw + KV_WIDTH].astype(BF16)
        for kh in range(N_KV_HEADS):
            kcols = slice(qw + kh * HEAD_DIM, qw + (kh + 1) * HEAD_DIM)
            vcols = slice(qw + KV_WIDTH + kh * HEAD_DIM, qw + KV_WIDTH + (kh + 1) * HEAD_DIM)
            va_scr[rows, 2 * kh * HEAD_DIM:(2 * kh + 1) * HEAD_DIM] = z[:, vcols].astype(BF16)
            va_scr[rows, (2 * kh + 1) * HEAD_DIM:(2 * kh + 2) * HEAD_DIM] = ones
            if state is not None:
                state_rows = pl.ds(N_KV_HEADS * r * ROW_BLOCK + kh, ROW_BLOCK, stride=N_KV_HEADS)
                state[0][state_rows, :] = z[:, kcols]
                state[1][state_rows, :] = z[:, vcols]

    def sink_rows(kh, rows_per_head):
        return jnp.concatenate(
            [jnp.full((rows_per_head, HEAD_DIM), sink(kh * KV_GROUP + g) * LOG2E, F32)
             for g in range(KV_GROUP)], axis=0)

    def stacked_queries(rows, kh):
        return jnp.concatenate(
            [q_scr[rows, (kh * KV_GROUP + g) * HEAD_DIM:(kh * KV_GROUP + g + 1) * HEAD_DIM]
             for g in range(KV_GROUP)], axis=0)

    def store_heads(rows, kh, out, rows_per_head):
        for g in range(KV_GROUP):
            hd = kh * KV_GROUP + g
            a_scr[rows, hd * HEAD_DIM:(hd + 1) * HEAD_DIM] = (
                out[g * rows_per_head:(g + 1) * rows_per_head].astype(BF16))

    if latent is not None:
        nblk = seq_len // Q_BLOCK
        assert nblk >= 3 and WINDOW == Q_BLOCK and rows_total == seq_len
        past = ck_ref.shape[1] // N_KV_HEADS
        for kh in range(N_KV_HEADS):
            cache_rows = pl.ds(kh, past, stride=N_KV_HEADS)
            cka_scr[:, kh * HEAD_DIM:(kh + 1) * HEAD_DIM] = ck_ref[0, cache_rows, :].astype(BF16)
            cva_scr[:, 2 * kh * HEAD_DIM:(2 * kh + 1) * HEAD_DIM] = cv_ref[0, cache_rows, :].astype(BF16)
            cva_scr[:, (2 * kh + 1) * HEAD_DIM:(2 * kh + 2) * HEAD_DIM] = jnp.ones((past, HEAD_DIM), BF16)
        r_idx = lax.broadcasted_iota(jnp.int32, (KV_GROUP * Q_BLOCK, 3 * Q_BLOCK), 0) % Q_BLOCK
        c_idx = lax.broadcasted_iota(jnp.int32, (KV_GROUP * Q_BLOCK, 3 * Q_BLOCK), 1)
        bias_scr[0] = jnp.where(c_idx - r_idx <= WINDOW, 0.0, NEG)
        bias_scr[1] = jnp.where((c_idx >= r_idx) & (c_idx - r_idx <= 2 * WINDOW), 0.0, NEG)
        bias_scr[2] = jnp.where(c_idx >= r_idx, 0.0, NEG)
        for j in range(nblk):
            rows = pl.ds(j * Q_BLOCK, Q_BLOCK)
            lo, hi = max(j - 1, 0), min(j + 2, nblk)
            band = pl.ds(lo * Q_BLOCK, (hi - lo) * Q_BLOCK)
            nk = (hi - lo) * Q_BLOCK
            variant = 0 if j == 0 else (2 if j == nblk - 1 else 1)
            for kh in range(N_KV_HEADS):
                kl = slice(kh * HEAD_DIM, (kh + 1) * HEAD_DIM)
                vl = slice(2 * kh * HEAD_DIM, (2 * kh + 2) * HEAD_DIM)
                qh = stacked_queries(rows, kh)
                s_band = _dot_t(qh, k_scr[band, kl]) + bias_scr[variant, :, 0:nk]
                s_ctx = _dot_t(qh, cka_scr[:, kl])
                out = _softmax_pv([s_band, s_ctx], [va_scr[band, vl], cva_scr[:, vl]],
                                  sink_rows(kh, Q_BLOCK))
                store_heads(rows, kh, out, Q_BLOCK)
    else:
        for s in range(rows_total // seq_len):
            rows = pl.ds(s * seq_len, seq_len)
            for kh in range(N_KV_HEADS):
                kl = slice(kh * HEAD_DIM, (kh + 1) * HEAD_DIM)
                vl = slice(2 * kh * HEAD_DIM, (2 * kh + 2) * HEAD_DIM)
                qh = stacked_queries(rows, kh)
                out = _softmax_pv([_dot_t(qh, k_scr[rows, kl])], [va_scr[rows, vl]], sink_rows(kh, seq_len))
                store_heads(rows, kh, out, seq_len)

    for r in range(rows_total // ROW_BLOCK):
        rows = pl.ds(r * ROW_BLOCK, ROW_BLOCK)
        y = _dot(a_scr[rows, :], wo_ref[...])
        o_ref[rows, :] = _gated_residual(x_ref[rows, :], y, post_gain)


def _attn_kernel(*refs, layer, widx, prompt_tiles, prompt_seq, latent_seq, n_x):
    sink_ref, x_refs, refs = refs[0], refs[1:1 + n_x], refs[1 + n_x:]
    (mod_ref, npre_ref, npost_ref, wqkv_hbm, wo_hbm, cos_ref, sup_ref, sdn_ref, ck_ref, cv_ref,
     o_ref, ko_ref, vo_ref,
     wqkv_ref, wo_ref, stage, sem, q_scr, k_scr, va_scr, a_scr, cka_scr, cva_scr, bias_scr) = refs
    step = pl.program_id(0)

    @pl.when(step == 0)
    def _():
        _stage_weights([(wqkv_hbm, widx, wqkv_ref), (wo_hbm, widx, wo_ref)], stage, sem)

    shift, scale, gate = mod_ref[0, 0, 0:1, :], mod_ref[0, 0, 1:2, :], mod_ref[0, 0, 2:3, :]
    pre_gain = npre_ref[layer:layer + 1, :] * (1.0 + scale)
    post_gain = npost_ref[layer:layer + 1, :] * gate
    common = (o_ref, shift, pre_gain, post_gain, lambda h: sink_ref[widx, h],
              wqkv_ref, wo_ref, q_scr, k_scr, va_scr, a_scr)

    @pl.when(step < prompt_tiles)
    def _():
        _attn_tile(x_refs[0], *common, prompt_seq, state=(ko_ref, vo_ref))

    @pl.when(step >= prompt_tiles)
    def _():
        _attn_tile(x_refs[-1], *common, latent_seq,
                   latent=(cos_ref, sup_ref, sdn_ref, ck_ref, cv_ref, cka_scr, cva_scr, bias_scr))


def _attn_call(x, n_prompt, n_latent, mods, sink, npre, npost, wqkv, wo, cache_k, cache_v,
               prompt_seq, latent_seq, layer, widx):
    d = wqkv.shape[1]
    tp, ts = n_prompt // ROW_TILE, n_latent // ROW_TILE
    past = cache_k.shape[1] // N_KV_HEADS
    cos, sup, sdn = _rope_tables(latent_seq)
    x_specs, x_args = _stream_specs(x, tp, d)
    cache_spec = pl.BlockSpec((1,) + cache_k.shape[1:], lambda i: (jnp.maximum(i - tp, 0), 0, 0))
    state_spec = pl.BlockSpec((N_KV_HEADS * ROW_TILE, HEAD_DIM), lambda i: (jnp.minimum(i, tp - 1), 0))
    return pl.pallas_call(
        functools.partial(_attn_kernel, layer=layer, widx=widx, prompt_tiles=tp,
                          prompt_seq=prompt_seq, latent_seq=latent_seq, n_x=len(x_args)),
        grid=(tp + ts,),
        in_specs=[pl.BlockSpec(memory_space=pltpu.SMEM)] + x_specs + [
            _mod_spec(layer, tp, d),
            _const_spec(npre.shape), _const_spec(npost.shape), HBM_SPEC, HBM_SPEC,
            _const_spec(cos.shape), _const_spec(sup.shape), _const_spec(sdn.shape), cache_spec, cache_spec,
        ],
        out_specs=[pl.BlockSpec((ROW_TILE, d), lambda i: (i, 0)), state_spec, state_spec],
        out_shape=[jax.ShapeDtypeStruct((n_prompt + n_latent, d), F32),
                   jax.ShapeDtypeStruct((N_KV_HEADS * n_prompt, HEAD_DIM), F32),
                   jax.ShapeDtypeStruct((N_KV_HEADS * n_prompt, HEAD_DIM), F32)],
        scratch_shapes=[
            pltpu.VMEM(wqkv.shape[1:], BF16),
            pltpu.VMEM(wo.shape[1:], BF16),
            *_staging_scratch(by_cols=True),
            pltpu.VMEM((ROW_TILE, N_HEADS * HEAD_DIM), BF16),
            pltpu.VMEM((ROW_TILE, KV_WIDTH), BF16),
            pltpu.VMEM((ROW_TILE, 2 * KV_WIDTH), BF16),
            pltpu.VMEM((ROW_TILE, N_HEADS * HEAD_DIM), BF16),
            pltpu.VMEM((past, KV_WIDTH), BF16),
            pltpu.VMEM((past, 2 * KV_WIDTH), BF16),
            pltpu.VMEM((3, KV_GROUP * Q_BLOCK, 3 * Q_BLOCK), F32),
        ],
        compiler_params=pltpu.CompilerParams(
            dimension_semantics=("arbitrary",), vmem_limit_bytes=VMEM_LIMIT),
        name="attn",
    )(sink, *x_args, mods, npre, npost, wqkv, wo, cos, sup, sdn, cache_k, cache_v)


def _ffn_kernel(*refs, layer, prompt_tiles, split_out):
    if split_out:
        (x_ref, mod_ref, npre_ref, npost_ref, wg_hbm, wu_hbm, wd_hbm, op_ref, os_ref,
         wg_ref, wu_ref, wd_ref, col_stage, col_sem, row_stage, row_sem, a_scr) = refs
    else:
        (x_ref, mod_ref, npre_ref, npost_ref, wg_hbm, wu_hbm, wd_hbm, o_ref,
         wg_ref, wu_ref, wd_ref, col_stage, col_sem, row_stage, row_sem, a_scr) = refs
    step = pl.program_id(0)
    shift, scale, gate = mod_ref[0, 0, 3:4, :], mod_ref[0, 0, 4:5, :], mod_ref[0, 0, 5:6, :]
    pre_gain = npre_ref[layer:layer + 1, :] * (1.0 + scale)
    post_gain = npost_ref[layer:layer + 1, :] * gate
    d_ff = wg_ref.shape[1]
    chunks = [(c0, min(c0 + FFN_CHUNK, d_ff)) for c0 in range(0, d_ff, FFN_CHUNK)]
    block = x_ref.shape[0] if split_out else FFN_BLOCK
    blocks = [pl.ds(b * block, block) for b in range(x_ref.shape[0] // block)]

    def gate_up(h, rows, c0, c1):
        g = _dot(h, wg_ref[:, c0:c1])
        u = _dot(h, wu_ref[:, c0:c1])
        a_scr[rows, c0:c1] = (jax.nn.silu(g) * u).astype(BF16)

    def down(rows):
        y = _dot(a_scr[rows, :], wd_ref[...])
        return _gated_residual(x_ref[rows, :], y, post_gain)

    @pl.when(step == 0)
    def _():
        row = _WeightStager(_weight_slabs(wd_hbm, layer, wd_ref, False), row_stage, row_sem)
        row.advance(0)
        _stage_weights([(wg_hbm, layer, wg_ref), (wu_hbm, layer, wu_ref)], col_stage, col_sem)
        row.finish()

    for rows in blocks:
        h = _modulated_norm(x_ref[rows, :], pre_gain, shift)
        for c0, c1 in chunks:
            gate_up(h, rows, c0, c1)
        out = down(rows)
        if not split_out:
            o_ref[rows, :] = out
    if split_out:
        @pl.when(step < prompt_tiles)
        def _():
            op_ref[...] = out

        @pl.when(step >= prompt_tiles)
        def _():
            os_ref[...] = out


def _ffn_call(x, n_prompt, n_latent, mods, npre, npost, wg, wu, wd, layer, latent_seq, split_out):
    d, d_ff = wg.shape[1], wg.shape[2]
    tile = FFN_SPLIT_TILE if split_out else ROW_TILE
    tp, ts = n_prompt // tile, n_latent // tile
    row_spec = pl.BlockSpec((tile, d), lambda i: (i, 0))
    if split_out:
        out_specs = [pl.BlockSpec((tile, d), lambda i: (jnp.minimum(i, tp - 1), 0)),
                     pl.BlockSpec((tile, d), lambda i: (jnp.maximum(i - tp, 0), 0))]
        out_shape = [jax.ShapeDtypeStruct((n_prompt, d), F32), jax.ShapeDtypeStruct((n_latent, d), F32)]
    else:
        out_specs = row_spec
        out_shape = jax.ShapeDtypeStruct((n_prompt + n_latent, d), F32)
    return pl.pallas_call(
        functools.partial(_ffn_kernel, layer=layer, prompt_tiles=tp, split_out=split_out),
        grid=(tp + ts,),
        in_specs=[row_spec, _mod_spec(layer, tp, d, latent_seq // tile),
                  _const_spec(npre.shape), _const_spec(npost.shape), HBM_SPEC, HBM_SPEC, HBM_SPEC],
        out_specs=out_specs,
        out_shape=out_shape,
        scratch_shapes=[
            pltpu.VMEM((d, d_ff), BF16), pltpu.VMEM((d, d_ff), BF16), pltpu.VMEM((d_ff, d), BF16),
            *_staging_scratch(by_cols=True), *_staging_scratch(by_cols=False),
            pltpu.VMEM((tile, d_ff), BF16),
        ],
        compiler_params=pltpu.CompilerParams(
            dimension_semantics=("arbitrary",), vmem_limit_bytes=VMEM_LIMIT),
        name="ffn",
    )(x, mods, npre, npost, wg, wu, wd)


def kernel(x_prompt, x_sample, cache_k, cache_v, c, c_ctx, mod_w, mod_b, norm_pre_mix, norm_post_mix,
           norm_pre_ffn, norm_post_ffn, ab_w_in, sgu_w, sgu_b, sgu_g, ab_w_out, attn_w_qkv, attn_sink,
           attn_w_o, ffn_w_gate, ffn_w_up, ffn_w_down):
    bp, sp, d = x_prompt.shape
    bs, ss, _ = x_sample.shape
    depth = mod_w.shape[0]
    n_prompt, n_latent = bp * sp, bs * ss
    assert d == D_MODEL and ROW_TILE % sp == 0 and ss == ROW_TILE and n_prompt % ROW_TILE == 0
    assert 1 + bs <= MOD_ROWS

    cond = jnp.zeros((MOD_ROWS, d), F32).at[0].set(c_ctx).at[1:1 + bs].set(c)
    mods = _mod_call(cond, mod_w, mod_b).reshape(depth, MOD_ROWS, 6, d)

    x = (x_prompt.reshape(n_prompt, d), x_sample.reshape(n_latent, d))
    state_k, state_v = [], []
    for layer in range(depth):
        if layer % 2 == 0:
            e = layer // 2
            sgub_full = jnp.repeat(sgu_b[e].T, A_WIDTH // A_GROUPS, axis=1)
            x = _mixer_ab_call(x, n_prompt, n_latent, mods, norm_pre_mix, norm_post_mix, ab_w_in, sgu_w,
                               sgub_full, sgu_g, ab_w_out, sp, ss, layer, e)
        else:
            o = layer // 2
            ck = cache_k[:, o].reshape(bs, -1, HEAD_DIM)
            cv = cache_v[:, o].reshape(bs, -1, HEAD_DIM)
            x, kp, vp = _attn_call(x, n_prompt, n_latent, mods, attn_sink, norm_pre_mix, norm_post_mix,
                                   attn_w_qkv, attn_w_o, ck, cv, sp, ss, layer, o)
            state_k.append(kp.reshape(bp, sp, N_KV_HEADS, HEAD_DIM))
            state_v.append(vp.reshape(bp, sp, N_KV_HEADS, HEAD_DIM))
        x = _ffn_call(x, n_prompt, n_latent, mods, norm_pre_ffn, norm_post_ffn,
                      ffn_w_gate, ffn_w_up, ffn_w_down, layer, ss, split_out=(layer == depth - 1))
    xp, xs = x
    return (xp.reshape(bp, sp, d), xs.reshape(bs, ss, d),
            jnp.stack(state_k, axis=1), jnp.stack(state_v, axis=1))
```

```python
import functools

import numpy as np
import jax
import jax.numpy as jnp
from jax import lax
from jax.experimental import pallas as pl
from jax.experimental.pallas import tpu as pltpu

F32 = jnp.float32
BF16 = jnp.bfloat16

D_MODEL = 1024
GRID_W = 64
CHUNK = 128
A_GROUPS = 4
A_WIDTH = D_MODEL // 2
B_GROUPS = 4
B_WIDTH = D_MODEL // 2
B_GROUP_DIM = B_WIDTH // B_GROUPS
HEAD_DIM = 128
N_HEADS = D_MODEL // HEAD_DIM
N_KV_HEADS = 2
KV_GROUP = N_HEADS // N_KV_HEADS
KV_WIDTH = N_KV_HEADS * HEAD_DIM
WINDOW = 128
Q_BLOCK = 128
AXIS_DIM = HEAD_DIM // 2
ROPE_BASE = 10000.0
EPS = 1e-6
NEG = -1e30
LOG2E = 1.4426950408889634

ROW_TILE = 1024
ROW_BLOCK = 512
FFN_BLOCK = 512
FFN_SPLIT_TILE = 512
FFN_CHUNK = 768
MOD_ROWS = 16
MOD_COLS = 1536
WEIGHT_CHUNK = 128
STAGE_SLOTS = 8
VMEM_LIMIT = 56 * 1024 * 1024


def _dot(a, b):
    return jnp.dot(a, b, preferred_element_type=F32)


def _dot_t(a, b):
    return lax.dot_general(a, b, (((1,), (1,)), ((), ())), preferred_element_type=F32)


def _modulated_norm(x, gain, shift):
    ms = jnp.mean(x * x, axis=-1, keepdims=True)
    return (x * lax.rsqrt(ms + EPS) * gain + shift).astype(BF16)


def _gated_residual(x, y, gain):
    ms = jnp.mean(y * y, axis=-1, keepdims=True)
    return x + y * lax.rsqrt(ms + EPS) * gain


def _const_spec(shape):
    nd = len(shape)
    return pl.BlockSpec(shape, lambda i, _nd=nd: (0,) * _nd)


HBM_SPEC = pl.BlockSpec(memory_space=pl.ANY)


def _staging_scratch(by_cols):
    shape = (STAGE_SLOTS, D_MODEL, WEIGHT_CHUNK) if by_cols else (STAGE_SLOTS, WEIGHT_CHUNK, D_MODEL)
    return [pltpu.VMEM(shape, F32), pltpu.SemaphoreType.DMA((STAGE_SLOTS,))]


class _WeightStager:
    def __init__(self, slabs, stage, sem):
        self.slabs, self.stage, self.sem = slabs, stage, sem
        self.by_cols = stage.shape[2] == WEIGHT_CHUNK
        self.started = self.done = 0

    def _copy(self, i):
        w_hbm, layer, _, index = self.slabs[i]
        span = pl.ds(index * WEIGHT_CHUNK, WEIGHT_CHUNK)
        src = w_hbm.at[layer, :, span] if self.by_cols else w_hbm.at[layer, span, :]
        return pltpu.make_async_copy(src, self.stage.at[i % STAGE_SLOTS], self.sem.at[i % STAGE_SLOTS])

    def _start_ahead(self):
        while self.started < min(self.done + STAGE_SLOTS, len(self.slabs)):
            self._copy(self.started).start()
            self.started += 1

    def advance(self, upto):
        self._start_ahead()
        while self.done < upto:
            i = self.done
            self._copy(i).wait()
            _, _, dst, index = self.slabs[i]
            lo, hi = index * WEIGHT_CHUNK, (index + 1) * WEIGHT_CHUNK
            if self.by_cols:
                dst[:, lo:hi] = self.stage[i % STAGE_SLOTS].astype(BF16)
            else:
                dst[lo:hi, :] = self.stage[i % STAGE_SLOTS].astype(BF16)
            self.done += 1
            self._start_ahead()

    def finish(self):
        self.advance(len(self.slabs))


def _weight_slabs(w_hbm, layer, dst, by_cols, first=0, last=None):
    size = dst.shape[1] if by_cols else dst.shape[0]
    assert size % WEIGHT_CHUNK == 0
    last = size // WEIGHT_CHUNK if last is None else last
    return [(w_hbm, layer, dst, i) for i in range(first, last)]


def _stage_weights(weights, stage, sem):
    by_cols = stage.shape[2] == WEIGHT_CHUNK
    slabs = [s for w_hbm, layer, dst in weights for s in _weight_slabs(w_hbm, layer, dst, by_cols)]
    _WeightStager(slabs, stage, sem).finish()


def _stream_specs(x, prompt_tiles, d):
    tp = prompt_tiles
    if isinstance(x, tuple):
        return ([pl.BlockSpec((ROW_TILE, d), lambda i: (jnp.minimum(i, tp - 1), 0)),
                 pl.BlockSpec((ROW_TILE, d), lambda i: (jnp.maximum(i - tp, 0), 0))], list(x))
    return [pl.BlockSpec((ROW_TILE, d), lambda i: (i, 0))], [x]


def _mod_spec(layer, prompt_tiles, d, tiles_per_request=1):
    n = tiles_per_request
    return pl.BlockSpec((1, 1, 6, d), lambda i: (layer, jnp.maximum(i - prompt_tiles + n, 0) // n, 0, 0))


def _mod_kernel(cond_ref, w_ref, b_ref, o_ref):
    a = jax.nn.silu(cond_ref[...]).astype(BF16)
    o_ref[0] = _dot(a, w_ref[0].astype(BF16)) + b_ref[0]


def _mod_call(cond, mod_w, mod_b):
    depth, d, n = mod_w.shape
    return pl.pallas_call(
        _mod_kernel,
        grid=(depth, n // MOD_COLS),
        in_specs=[
            pl.BlockSpec((MOD_ROWS, d), lambda l, j: (0, 0)),
            pl.BlockSpec((1, d, MOD_COLS), lambda l, j: (l, 0, j)),
            pl.BlockSpec((1, 1, MOD_COLS), lambda l, j: (l, 0, j)),
        ],
        out_specs=pl.BlockSpec((1, MOD_ROWS, MOD_COLS), lambda l, j: (l, 0, j)),
        out_shape=jax.ShapeDtypeStruct((depth, MOD_ROWS, n), F32),
        compiler_params=pltpu.CompilerParams(
            dimension_semantics=("arbitrary", "arbitrary"), vmem_limit_bytes=VMEM_LIMIT),
        name="modulation",
    )(cond, mod_w, mod_b.reshape(depth, 1, n))


def _dft_tables(t):
    c = B_GROUP_DIM
    kc = (np.arange(c)[:, None] * np.arange(c)[None, :]) % c
    ac = 2.0 * np.pi * kc / c
    fc = np.concatenate([np.cos(ac), np.sin(ac)], axis=1)
    kt = (np.arange(t)[:, None] * np.arange(t)[None, :]) % t
    at = 2.0 * np.pi * kt / t
    ft = np.concatenate([np.cos(at), -np.sin(at)], axis=1) / np.sqrt(float(t * c))
    return jnp.asarray(fc, dtype=F32).astype(BF16), jnp.asarray(ft, dtype=F32).astype(BF16)


def _mixer_ab_tile(x_ref, o_ref, shift, pre_gain, post_gain, win_ref, sguw_ref, sgub_ref, sgug, fc_ref, ft_ref,
                   wout_ref, u_scr, v_scr, zb_scr, ab_scr, zcs_scr, seq_len):
    rows_total = x_ref.shape[0]

    for r in range(rows_total // ROW_BLOCK):
        rows = pl.ds(r * ROW_BLOCK, ROW_BLOCK)
        h = _modulated_norm(x_ref[rows, :], pre_gain, shift)
        z = _dot(h, win_ref[...])
        u_scr[rows, :] = jax.nn.gelu(z[:, :A_WIDTH])
        v = jax.nn.gelu(z[:, A_WIDTH:2 * A_WIDTH])
        vc = v - jnp.mean(v, axis=-1, keepdims=True)
        v = vc * lax.rsqrt(jnp.mean(vc * vc, axis=-1, keepdims=True) + EPS) * sgug
        v_scr[rows, :] = v.astype(BF16)
        zb_scr[rows, :] = z[:, 2 * A_WIDTH:].astype(BF16)

    gw = A_WIDTH // A_GROUPS
    for n in range(rows_total // CHUNK):
        rows = pl.ds(n * CHUNK, CHUNK)
        for g in range(A_GROUPS):
            lanes = slice(g * gw, (g + 1) * gw)
            mixed = _dot(sguw_ref[g].astype(BF16), v_scr[rows, lanes]) + sgub_ref[:, lanes]
            ab_scr[rows, lanes] = (u_scr[rows, lanes] * mixed).astype(BF16)

    c = B_GROUP_DIM
    for s in range(rows_total // seq_len):
        rows = pl.ds(s * seq_len, seq_len)
        cs_rows = pl.ds(2 * s * seq_len, 2 * seq_len)
        for g in range(B_GROUPS):
            lanes = slice(g * c, (g + 1) * c)
            zz = _dot(zb_scr[rows, lanes], fc_ref[...])
            zcs_scr[pl.ds(2 * s * seq_len, seq_len), lanes] = zz[:, :c].astype(BF16)
            zcs_scr[pl.ds((2 * s + 1) * seq_len, seq_len), lanes] = zz[:, c:].astype(BF16)
        ab_scr[rows, A_WIDTH:] = _dot(ft_ref[...], zcs_scr[cs_rows, :]).astype(BF16)

    for r in range(rows_total // ROW_BLOCK):
        rows = pl.ds(r * ROW_BLOCK, ROW_BLOCK)
        y = _dot(ab_scr[rows, :], wout_ref[...])
        o_ref[rows, :] = _gated_residual(x_ref[rows, :], y, post_gain)


def _mixer_ab_kernel(*refs, layer, widx, prompt_tiles, prompt_seq, latent_seq, n_x):
    x_refs, refs = refs[:n_x], refs[n_x:]
    (mod_ref, npre_ref, npost_ref, win_hbm, sguw_ref, sgub_ref, sgug_ref, fc_ref, ftp_ref, fts_ref, wout_hbm,
     o_ref, win_ref, wout_ref, stage, sem, u_scr, v_scr, zb_scr, ab_scr, zcs_scr) = refs
    step = pl.program_id(0)

    @pl.when(step == 0)
    def _():
        _stage_weights([(win_hbm, widx, win_ref), (wout_hbm, widx, wout_ref)], stage, sem)

    shift, scale, gate = mod_ref[0, 0, 0:1, :], mod_ref[0, 0, 1:2, :], mod_ref[0, 0, 2:3, :]
    pre_gain = npre_ref[layer:layer + 1, :] * (1.0 + scale)
    post_gain = npost_ref[layer:layer + 1, :] * gate
    common = (o_ref, shift, pre_gain, post_gain, win_ref, sguw_ref.at[widx], sgub_ref, sgug_ref[widx:widx + 1, :],
              fc_ref)
    scratch = (wout_ref, u_scr, v_scr, zb_scr, ab_scr, zcs_scr)

    @pl.when(step < prompt_tiles)
    def _():
        _mixer_ab_tile(x_refs[0], *common, ftp_ref, *scratch, prompt_seq)

    @pl.when(step >= prompt_tiles)
    def _():
        _mixer_ab_tile(x_refs[-1], *common, fts_ref, *scratch, latent_seq)


def _mixer_ab_call(x, n_prompt, n_latent, mods, npre, npost, win, sguw, sgub_full, sgug, wout,
                   prompt_seq, latent_seq, layer, widx):
    d = win.shape[1]
    tp, ts = n_prompt // ROW_TILE, n_latent // ROW_TILE
    fc, ftp = _dft_tables(prompt_seq)
    _, fts = _dft_tables(latent_seq)
    x_specs, x_args = _stream_specs(x, tp, d)
    return pl.pallas_call(
        functools.partial(_mixer_ab_kernel, layer=layer, widx=widx, prompt_tiles=tp,
                          prompt_seq=prompt_seq, latent_seq=latent_seq, n_x=len(x_args)),
        grid=(tp + ts,),
        in_specs=x_specs + [
            _mod_spec(layer, tp, d),
            _const_spec(npre.shape), _const_spec(npost.shape), HBM_SPEC,
            _const_spec(sguw.shape), _const_spec(sgub_full.shape), _const_spec(sgug.shape),
            _const_spec(fc.shape), _const_spec(ftp.shape), _const_spec(fts.shape), HBM_SPEC,
        ],
        out_specs=pl.BlockSpec((ROW_TILE, d), lambda i: (i, 0)),
        out_shape=jax.ShapeDtypeStruct((n_prompt + n_latent, d), F32),
        scratch_shapes=[
            pltpu.VMEM(win.shape[1:], BF16),
            pltpu.VMEM(wout.shape[1:], BF16),
            *_staging_scratch(by_cols=True),
            pltpu.VMEM((ROW_TILE, A_WIDTH), F32),
            pltpu.VMEM((ROW_TILE, A_WIDTH), BF16),
            pltpu.VMEM((ROW_TILE, B_WIDTH), BF16),
            pltpu.VMEM((ROW_TILE, A_WIDTH + B_WIDTH), BF16),
            pltpu.VMEM((2 * ROW_TILE, B_WIDTH), BF16),
        ],
        compiler_params=pltpu.CompilerParams(
            dimension_semantics=("arbitrary",), vmem_limit_bytes=VMEM_LIMIT),
        name="mixer_ab",
    )(*x_args, mods, npre, npost, win, sguw, sgub_full, sgug, fc, ftp, fts, wout)


def _rope_tables(t):
    rows_n = t // GRID_W
    rows = jnp.repeat(jnp.arange(rows_n), GRID_W).astype(F32)
    cols = jnp.tile(jnp.arange(GRID_W), rows_n).astype(F32)
    inv = ROPE_BASE ** (-jnp.arange(0, AXIS_DIM, 2, dtype=F32) / AXIS_DIM)
    ar = rows[:, None] * inv
    ac = cols[:, None] * inv
    ang = jnp.concatenate([ar, ar, ac, ac], axis=-1)
    cos, sin = jnp.cos(ang), jnp.sin(ang)
    first_half = (jnp.arange(HEAD_DIM) % AXIS_DIM) < (AXIS_DIM // 2)
    sin_up = jnp.where(first_half[None, :], -sin, 0.0)
    sin_dn = jnp.where(first_half[None, :], 0.0, sin)
    return cos, sin_up, sin_dn


def _softmax_pv(score_parts, value_parts, sink_rows):
    rows = score_parts[0].shape[0]
    folded = None
    for s in score_parts:
        for i in range(0, s.shape[1], HEAD_DIM):
            blk = s[:, i:i + HEAD_DIM]
            folded = blk if folded is None else jnp.maximum(folded, blk)
    m = jnp.max(folded, axis=-1, keepdims=True)
    mb = jnp.maximum(jnp.broadcast_to(m, (rows, HEAD_DIM)), sink_rows)
    acc = None
    for s, v in zip(score_parts, value_parts):
        p = jnp.concatenate(
            [jnp.exp2(s[:, i:i + HEAD_DIM] - mb) for i in range(0, s.shape[1], HEAD_DIM)], axis=1)
        o = _dot(p.astype(BF16), v)
        acc = o if acc is None else acc + o
    denom = acc[:, HEAD_DIM:] + jnp.exp2(sink_rows - mb)
    return acc[:, :HEAD_DIM] / denom


def _attn_tile(x_ref, o_ref, shift, pre_gain, post_gain, sink, wqkv_ref, wo_ref, q_scr, k_scr, va_scr, a_scr,
               seq_len, state=None, latent=None):
    rows_total = x_ref.shape[0]
    qw = N_HEADS * HEAD_DIM
    q_scale = HEAD_DIM ** -0.5 * LOG2E
    ones = jnp.ones((ROW_BLOCK, HEAD_DIM), BF16)
    if latent is not None:
        cos_ref, sup_ref, sdn_ref, ck_ref, cv_ref, cka_scr, cva_scr, bias_scr = latent

    for r in range(rows_total // ROW_BLOCK):
        rows = pl.ds(r * ROW_BLOCK, ROW_BLOCK)
        h = _modulated_norm(x_ref[rows, :], pre_gain, shift)
        z = _dot(h, wqkv_ref[...])
        if latent is not None:
            cos, sup, sdn = cos_ref[rows, :], sup_ref[rows, :], sdn_ref[rows, :]
            for hd in range(N_HEADS + N_KV_HEADS):
                zh = z[:, hd * HEAD_DIM:(hd + 1) * HEAD_DIM]
                zr = (zh * cos + pltpu.roll(zh, HEAD_DIM - AXIS_DIM // 2, axis=1) * sup
                      + pltpu.roll(zh, AXIS_DIM // 2, axis=1) * sdn)
                if hd < N_HEADS:
                    q_scr[rows, hd * HEAD_DIM:(hd + 1) * HEAD_DIM] = (zr * q_scale).astype(BF16)
                else:
                    k_scr[rows, (hd - N_HEADS) * HEAD_DIM:(hd - N_HEADS + 1) * HEAD_DIM] = zr.astype(BF16)
        else:
            q_scr[rows, :] = (z[:, :qw] * q_scale).astype(BF16)
            k_scr[rows, :] = z[:, qw:qw + KV_WIDTH].astype(BF16)
        for kh in range(N_KV_HEADS):
            kcols = slice(qw + kh * HEAD_DIM, qw + (kh + 1) * HEAD_DIM)
            vcols = slice(qw + KV_WIDTH + kh * HEAD_DIM, qw + KV_WIDTH + (kh + 1) * HEAD_DIM)
            va_scr[rows, 2 * kh * HEAD_DIM:(2 * kh + 1) * HEAD_DIM] = z[:, vcols].astype(BF16)
            va_scr[rows, (2 * kh + 1) * HEAD_DIM:(2 * kh + 2) * HEAD_DIM] = ones
            if state is not None:
                state_rows = pl.ds(N_KV_HEADS * r * ROW_BLOCK + kh, ROW_BLOCK, stride=N_KV_HEADS)
                state[0][state_rows, :] = z[:, kcols]
                state[1][state_rows, :] = z[:, vcols]

    def sink_rows(kh, rows_per_head):
        return jnp.concatenate(
            [jnp.full((rows_per_head, HEAD_DIM), sink(kh * KV_GROUP + g) * LOG2E, F32)
             for g in range(KV_GROUP)], axis=0)

    def stacked_queries(rows, kh):
        return jnp.concatenate(
            [q_scr[rows, (kh * KV_GROUP + g) * HEAD_DIM:(kh * KV_GROUP + g + 1) * HEAD_DIM]
             for g in range(KV_GROUP)], axis=0)

    def store_heads(rows, kh, out, rows_per_head):
        for g in range(KV_GROUP):
            hd = kh * KV_GROUP + g
            a_scr[rows, hd * HEAD_DIM:(hd + 1) * HEAD_DIM] = (
                out[g * rows_per_head:(g + 1) * rows_per_head].astype(BF16))

    if latent is not None:
        nblk = seq_len // Q_BLOCK
        assert nblk >= 3 and WINDOW == Q_BLOCK and rows_total == seq_len
        past = ck_ref.shape[1] // N_KV_HEADS
        for kh in range(N_KV_HEADS):
            cache_rows = pl.ds(kh, past, stride=N_KV_HEADS)
            cka_scr[:, kh * HEAD_DIM:(kh + 1) * HEAD_DIM] = ck_ref[0, cache_rows, :].astype(BF16)
            cva_scr[:, 2 * kh * HEAD_DIM:(2 * kh + 1) * HEAD_DIM] = cv_ref[0, cache_rows, :].astype(BF16)
            cva_scr[:, (2 * kh + 1) * HEAD_DIM:(2 * kh + 2) * HEAD_DIM] = jnp.ones((past, HEAD_DIM), BF16)
        r_idx = lax.broadcasted_iota(jnp.int32, (KV_GROUP * Q_BLOCK, 3 * Q_BLOCK), 0) % Q_BLOCK
        c_idx = lax.broadcasted_iota(jnp.int32, (KV_GROUP * Q_BLOCK, 3 * Q_BLOCK), 1)
        bias_scr[0] = jnp.where(c_idx - r_idx <= WINDOW, 0.0, NEG)
        bias_scr[1] = jnp.where((c_idx >= r_idx) & (c_idx - r_idx <= 2 * WINDOW), 0.0, NEG)
        bias_scr[2] = jnp.where(c_idx >= r_idx, 0.0, NEG)
        for j in range(nblk):
            rows = pl.ds(j * Q_BLOCK, Q_BLOCK)
            lo, hi = max(j - 1, 0), min(j + 2, nblk)
            band = pl.ds(lo * Q_BLOCK, (hi - lo) * Q_BLOCK)
            nk = (hi - lo) * Q_BLOCK
            variant = 0 if j == 0 else (2 if j == nblk - 1 else 1)
            for kh in range(N_KV_HEADS):
                kl = slice(kh * HEAD_DIM, (kh + 1) * HEAD_DIM)
                vl = slice(2 * kh * HEAD_DIM, (2 * kh + 2) * HEAD_DIM)
                qh = stacked_queries(rows, kh)
                s_band = _dot_t(qh, k_scr[band, kl]) + bias_scr[variant, :, 0:nk]
                s_ctx = _dot_t(qh, cka_scr[:, kl])
                out = _softmax_pv([s_band, s_ctx], [va_scr[band, vl], cva_scr[:, vl]],
                                  sink_rows(kh, Q_BLOCK))
                store_heads(rows, kh, out, Q_BLOCK)
    else:
        for s in range(rows_total // seq_len):
            rows = pl.ds(s * seq_len, seq_len)
            for kh in range(N_KV_HEADS):
                kl = slice(kh * HEAD_DIM, (kh + 1) * HEAD_DIM)
                vl = slice(2 * kh * HEAD_DIM, (2 * kh + 2) * HEAD_DIM)
                qh = stacked_queries(rows, kh)
                out = _softmax_pv([_dot_t(qh, k_scr[rows, kl])], [va_scr[rows, vl]], sink_rows(kh, seq_len))
                store_heads(rows, kh, out, seq_len)

    for r in range(rows_total // ROW_BLOCK):
        rows = pl.ds(r * ROW_BLOCK, ROW_BLOCK)
        y = _dot(a_scr[rows, :], wo_ref[...])
        o_ref[rows, :] = _gated_residual(x_ref[rows, :], y, post_gain)


def _attn_kernel(*refs, layer, widx, prompt_tiles, prompt_seq, latent_seq, n_x):
    sink_ref, x_refs, refs = refs[0], refs[1:1 + n_x], refs[1 + n_x:]
    (mod_ref, npre_ref, npost_ref, wqkv_hbm, wo_hbm, cos_ref, sup_ref, sdn_ref, ck_ref, cv_ref,
     o_ref, ko_ref, vo_ref,
     wqkv_ref, wo_ref, stage, sem, q_scr, k_scr, va_scr, a_scr, cka_scr, cva_scr, bias_scr) = refs
    step = pl.program_id(0)

    @pl.when(step == 0)
    def _():
        _stage_weights([(wqkv_hbm, widx, wqkv_ref), (wo_hbm, widx, wo_ref)], stage, sem)

    shift, scale, gate = mod_ref[0, 0, 0:1, :], mod_ref[0, 0, 1:2, :], mod_ref[0, 0, 2:3, :]
    pre_gain = npre_ref[layer:layer + 1, :] * (1.0 + scale)
    post_gain = npost_ref[layer:layer + 1, :] * gate
    common = (o_ref, shift, pre_gain, post_gain, lambda h: sink_ref[widx, h],
              wqkv_ref, wo_ref, q_scr, k_scr, va_scr, a_scr)

    @pl.when(step < prompt_tiles)
    def _():
        _attn_tile(x_refs[0], *common, prompt_seq, state=(ko_ref, vo_ref))

    @pl.when(step >= prompt_tiles)
    def _():
        _attn_tile(x_refs[-1], *common, latent_seq,
                   latent=(cos_ref, sup_ref, sdn_ref, ck_ref, cv_ref, cka_scr, cva_scr, bias_scr))


def _attn_call(x, n_prompt, n_latent, mods, sink, npre, npost, wqkv, wo, cache_k, cache_v,
               prompt_seq, latent_seq, layer, widx):
    d = wqkv.shape[1]
    tp, ts = n_prompt // ROW_TILE, n_latent // ROW_TILE
    past = cache_k.shape[1] // N_KV_HEADS
    cos, sup, sdn = _rope_tables(latent_seq)
    x_specs, x_args = _stream_specs(x, tp, d)
    cache_spec = pl.BlockSpec((1,) + cache_k.shape[1:], lambda i: (jnp.maximum(i - tp, 0), 0, 0))
    state_spec = pl.BlockSpec((N_KV_HEADS * ROW_TILE, HEAD_DIM), lambda i: (jnp.minimum(i, tp - 1), 0))
    return pl.pallas_call(
        functools.partial(_attn_kernel, layer=layer, widx=widx, prompt_tiles=tp,
                          prompt_seq=prompt_seq, latent_seq=latent_seq, n_x=len(x_args)),
        grid=(tp + ts,),
        in_specs=[pl.BlockSpec(memory_space=pltpu.SMEM)] + x_specs + [
            _mod_spec(layer, tp, d),
            _const_spec(npre.shape), _const_spec(npost.shape), HBM_SPEC, HBM_SPEC,
            _const_spec(cos.shape), _const_spec(sup.shape), _const_spec(sdn.shape), cache_spec, cache_spec,
        ],
        out_specs=[pl.BlockSpec((ROW_TILE, d), lambda i: (i, 0)), state_spec, state_spec],
        out_shape=[jax.ShapeDtypeStruct((n_prompt + n_latent, d), F32),
                   jax.ShapeDtypeStruct((N_KV_HEADS * n_prompt, HEAD_DIM), F32),
                   jax.ShapeDtypeStruct((N_KV_HEADS * n_prompt, HEAD_DIM), F32)],
        scratch_shapes=[
            pltpu.VMEM(wqkv.shape[1:], BF16),
            pltpu.VMEM(wo.shape[1:], BF16),
            *_staging_scratch(by_cols=True),
            pltpu.VMEM((ROW_TILE, N_HEADS * HEAD_DIM), BF16),
            pltpu.VMEM((ROW_TILE, KV_WIDTH), BF16),
            pltpu.VMEM((ROW_TILE, 2 * KV_WIDTH), BF16),
            pltpu.VMEM((ROW_TILE, N_HEADS * HEAD_DIM), BF16),
            pltpu.VMEM((past, KV_WIDTH), BF16),
            pltpu.VMEM((past, 2 * KV_WIDTH), BF16),
            pltpu.VMEM((3, KV_GROUP * Q_BLOCK, 3 * Q_BLOCK), F32),
        ],
        compiler_params=pltpu.CompilerParams(
            dimension_semantics=("arbitrary",), vmem_limit_bytes=VMEM_LIMIT),
        name="attn",
    )(sink, *x_args, mods, npre, npost, wqkv, wo, cos, sup, sdn, cache_k, cache_v)


def _ffn_kernel(*refs, layer, prompt_tiles, split_out):
    if split_out:
        (x_ref, xn_ref, mod_ref, modn_ref, npre_ref, npost_ref, wg_hbm, wu_hbm, wd_hbm, op_ref, os_ref,
         wg_ref, wu_ref, wd_ref, col_stage, col_sem, row_stage, row_sem, a_scr, h_scr) = refs
    else:
        (x_ref, xn_ref, mod_ref, modn_ref, npre_ref, npost_ref, wg_hbm, wu_hbm, wd_hbm, o_ref,
         wg_ref, wu_ref, wd_ref, col_stage, col_sem, row_stage, row_sem, a_scr, h_scr) = refs
    step = pl.program_id(0)
    shift, scale, gate = mod_ref[0, 0, 3:4, :], mod_ref[0, 0, 4:5, :], mod_ref[0, 0, 5:6, :]
    pre_gain = npre_ref[layer:layer + 1, :] * (1.0 + scale)
    post_gain = npost_ref[layer:layer + 1, :] * gate
    next_gain = npre_ref[layer:layer + 1, :] * (1.0 + modn_ref[0, 0, 4:5, :])
    next_shift = modn_ref[0, 0, 3:4, :]
    d_ff = wg_ref.shape[1]
    chunks = [(c0, min(c0 + FFN_CHUNK, d_ff)) for c0 in range(0, d_ff, FFN_CHUNK)]
    block = x_ref.shape[0] if split_out else FFN_BLOCK
    blocks = [pl.ds(b * block, block) for b in range(x_ref.shape[0] // block)]

    def gate_up(h, rows, c0, c1):
        g = _dot(h(), wg_ref[:, c0:c1])
        u = _dot(h(), wu_ref[:, c0:c1])
        a_scr[rows, c0:c1] = (jax.nn.silu(g) * u).astype(BF16)

    def down(rows):
        y = _dot(a_scr[rows, :], wd_ref[...])
        return _gated_residual(x_ref[rows, :], y, post_gain)

    @pl.when(step == 0)
    def _():
        row = _WeightStager(_weight_slabs(wd_hbm, layer, wd_ref, False), row_stage, row_sem)
        row.advance(0)
        _stage_weights([(wg_hbm, layer, wg_ref), (wu_hbm, layer, wu_ref)], col_stage, col_sem)
        row.finish()
        h_scr[...] = _modulated_norm(x_ref[blocks[0], :], pre_gain, shift)

    for bi, rows in enumerate(blocks):
        if bi == 0:
            h = lambda: h_scr[...]
        else:
            h_val = _modulated_norm(x_ref[rows, :], pre_gain, shift)
            h = lambda h_val=h_val: h_val
        for c0, c1 in chunks:
            gate_up(h, rows, c0, c1)
        if bi == len(blocks) - 1:
            h_scr[...] = _modulated_norm(xn_ref[...], next_gain, next_shift)
        out = down(rows)
        if not split_out:
            o_ref[rows, :] = out
    if split_out:
        @pl.when(step < prompt_tiles)
        def _():
            op_ref[...] = out

        @pl.when(step >= prompt_tiles)
        def _():
            os_ref[...] = out


def _ffn_call(x, n_prompt, n_latent, mods, npre, npost, wg, wu, wd, layer, latent_seq, split_out):
    d, d_ff = wg.shape[1], wg.shape[2]
    tile = FFN_SPLIT_TILE if split_out else ROW_TILE
    block = tile if split_out else FFN_BLOCK
    tp, ts = n_prompt // tile, n_latent // tile
    last = tp + ts - 1
    row_spec = pl.BlockSpec((tile, d), lambda i: (i, 0))
    next_spec = pl.BlockSpec((block, d), lambda i: (jnp.minimum(i + 1, last) * (tile // block), 0))
    mod_spec = _mod_spec(layer, tp, d, latent_seq // tile)
    next_mod_spec = pl.BlockSpec(mod_spec.block_shape, lambda i: mod_spec.index_map(jnp.minimum(i + 1, last)))
    if split_out:
        out_specs = [pl.BlockSpec((tile, d), lambda i: (jnp.minimum(i, tp - 1), 0)),
                     pl.BlockSpec((tile, d), lambda i: (jnp.maximum(i - tp, 0), 0))]
        out_shape = [jax.ShapeDtypeStruct((n_prompt, d), F32), jax.ShapeDtypeStruct((n_latent, d), F32)]
    else:
        out_specs = row_spec
        out_shape = jax.ShapeDtypeStruct((n_prompt + n_latent, d), F32)
    return pl.pallas_call(
        functools.partial(_ffn_kernel, layer=layer, prompt_tiles=tp, split_out=split_out),
        grid=(tp + ts,),
        in_specs=[row_spec, next_spec, mod_spec, next_mod_spec,
                  _const_spec(npre.shape), _const_spec(npost.shape), HBM_SPEC, HBM_SPEC, HBM_SPEC],
        out_specs=out_specs,
        out_shape=out_shape,
        scratch_shapes=[
            pltpu.VMEM((d, d_ff), BF16), pltpu.VMEM((d, d_ff), BF16), pltpu.VMEM((d_ff, d), BF16),
            *_staging_scratch(by_cols=True), *_staging_scratch(by_cols=False),
            pltpu.VMEM((tile, d_ff), BF16),
            pltpu.VMEM((block, d), BF16),
        ],
        compiler_params=pltpu.CompilerParams(
            dimension_semantics=("arbitrary",), vmem_limit_bytes=VMEM_LIMIT),
        name="ffn",
    )(x, x, mods, mods, npre, npost, wg, wu, wd)


def kernel(x_prompt, x_sample, cache_k, cache_v, c, c_ctx, mod_w, mod_b, norm_pre_mix, norm_post_mix,
           norm_pre_ffn, norm_post_ffn, ab_w_in, sgu_w, sgu_b, sgu_g, ab_w_out, attn_w_qkv, attn_sink,
           attn_w_o, ffn_w_gate, ffn_w_up, ffn_w_down):
    bp, sp, d = x_prompt.shape
    bs, ss, _ = x_sample.shape
    depth = mod_w.shape[0]
    n_prompt, n_latent = bp * sp, bs * ss
    assert d == D_MODEL and ROW_TILE % sp == 0 and ss == ROW_TILE and n_prompt % ROW_TILE == 0
    assert 1 + bs <= MOD_ROWS

    cond = jnp.zeros((MOD_ROWS, d), F32).at[0].set(c_ctx).at[1:1 + bs].set(c)
    mods = _mod_call(cond, mod_w, mod_b).reshape(depth, MOD_ROWS, 6, d)

    x = (x_prompt.reshape(n_prompt, d), x_sample.reshape(n_latent, d))
    state_k, state_v = [], []
    for layer in range(depth):
        if layer % 2 == 0:
            e = layer // 2
            sgub_full = jnp.repeat(sgu_b[e].T, A_WIDTH // A_GROUPS, axis=1)
            x = _mixer_ab_call(x, n_prompt, n_latent, mods, norm_pre_mix, norm_post_mix, ab_w_in, sgu_w,
                               sgub_full, sgu_g, ab_w_out, sp, ss, layer, e)
        else:
            o = layer // 2
            ck = cache_k[:, o].reshape(bs, -1, HEAD_DIM)
            cv = cache_v[:, o].reshape(bs, -1, HEAD_DIM)
            x, kp, vp = _attn_call(x, n_prompt, n_latent, mods, attn_sink, norm_pre_mix, norm_post_mix,
                                   attn_w_qkv, attn_w_o, ck, cv, sp, ss, layer, o)
            state_k.append(kp.reshape(bp, sp, N_KV_HEADS, HEAD_DIM))
            state_v.append(vp.reshape(bp, sp, N_KV_HEADS, HEAD_DIM))
        x = _ffn_call(x, n_prompt, n_latent, mods, norm_pre_ffn, norm_post_ffn,
                      ffn_w_gate, ffn_w_up, ffn_w_down, layer, ss, split_out=(layer == depth - 1))
    xp, xs = x
    return (xp.reshape(bp, sp, d), xs.reshape(bs, ss, d),
            jnp.stack(state_k, axis=1), jnp.stack(state_v, axis=1))
```
